```python
import math, functools
import jax, jax.numpy as jnp
from jax import lax
import numpy as np

D_MODEL = 1024
BATCH = 4
SEQ = 4096
DEPTH = 4
DEC_BATCH = 128
DEC_SEQ = 1
PAST_LEN = 2048
PAGE_SIZE = 128

N_MIXERS = 3
N_FOX = (DEPTH + 2) // 3
N_POOL_LAYERS = (DEPTH + 1) // 3
N_GLA = DEPTH // 3

FOX_HEADS = 16
FOX_HEAD_DIM = D_MODEL // FOX_HEADS
FOX_SCALE = FOX_HEAD_DIM ** -0.5
Q_BLOCK = 128

POOL_WINDOWS = (2, 4, 8, 16)
POOL_GROUPS = len(POOL_WINDOWS)
POOL_GROUP_DIM = D_MODEL // POOL_GROUPS
POOL_STATE = max(POOL_WINDOWS) - 1

GLA_HEADS = 4
GLA_DK = D_MODEL // 2 // GLA_HEADS
GLA_DV = D_MODEL // GLA_HEADS
GLA_RANK = 16
GLA_TAU = 16.0
GLA_CHUNK = 64

D_FF = -(-8 * D_MODEL // (3 * 256)) * 256
EPS = 1e-6

kernel_name = 'hybrid_fox_pool_gla_decoder'


def rms_norm(x, g):
    xf = x.astype(jnp.float32)
    y = xf * lax.rsqrt(jnp.mean(xf * xf, axis=-1, keepdims=True) + EPS)
    return (y * g.astype(jnp.float32)).astype(x.dtype)


def ada_split(c, w, b):
    return jnp.split(jax.nn.silu(c) @ w + b, 6, axis=-1)


def pre_norm(x, g, shift, scale):
    return rms_norm(x, g) * (1 + scale[:, None, :]) + shift[:, None, :]


def swiglu(h, wg, wu, wd):
    return (jax.nn.silu(h @ wg) * (h @ wu)) @ wd


def fox_project(h, wq, wk, wv, wf, bf, qn, kn):
    B, L, _ = h.shape
    shp = (B, L, FOX_HEADS, FOX_HEAD_DIM)
    q = rms_norm((h @ wq).reshape(shp), qn)
    k = rms_norm((h @ wk).reshape(shp), kn)
    v = (h @ wv).reshape(shp)
    logf = jax.nn.log_sigmoid((h @ wf + bf).astype(jnp.float32))
    return q, k, v, logf


def fox_logits(q, k, Fq, Fk):
    s = jnp.einsum('bqhd,bkhd->bhqk', q, k).astype(jnp.float32) * FOX_SCALE
    return s + jnp.swapaxes(Fq, 1, 2)[..., :, None] - jnp.swapaxes(Fk, 1, 2)[..., None, :]


def fox_prompt(h, wq, wk, wv, wf, bf, qn, kn, wo):
    q, k, v, logf = fox_project(h, wq, wk, wv, wf, bf, qn, kn)
    B, L, _ = h.shape
    nb = L // Q_BLOCK
    F = jnp.cumsum(logf, axis=1)
    k_pos = jnp.arange(L)
    qb = q.reshape(B, nb, Q_BLOCK, FOX_HEADS, FOX_HEAD_DIM).swapaxes(0, 1)
    Fb = F.reshape(B, nb, Q_BLOCK, FOX_HEADS).swapaxes(0, 1)

    def attend_block(args):
        qi, Fi, bi = args
        q_pos = bi * Q_BLOCK + jnp.arange(Q_BLOCK)
        s = fox_logits(qi, k, Fi, F)
        s = jnp.where(k_pos[None, :] <= q_pos[:, None], s, -jnp.inf)
        p = jax.nn.softmax(s, axis=-1).astype(v.dtype)
        return jnp.einsum('bhqk,bkhd->bqhd', p, v)

    o = lax.map(attend_block, (qb, Fb, jnp.arange(nb)))
    o = o.swapaxes(0, 1).reshape(B, L, D_MODEL)
    return o @ wo, k, v, logf


def fox_sample(h, cache_k, cache_v, cache_logf, j, page_table, wq, wk, wv, wf, bf, qn, kn, wo):
    q, k, v, logf = fox_project(h, wq, wk, wv, wf, bf, qn, kn)
    Bd, Lq, _ = h.shape
    P = page_table.shape[1] * PAGE_SIZE
    pk = cache_k[j, page_table].reshape(Bd, P, FOX_HEADS, FOX_HEAD_DIM).astype(q.dtype)
    pv = cache_v[j, page_table].reshape(Bd, P, FOX_HEADS, FOX_HEAD_DIM).astype(v.dtype)
    plf = cache_logf[j, page_table].reshape(Bd, P, FOX_HEADS).astype(jnp.float32)
    F_past = jnp.cumsum(plf, axis=1)
    F_new = F_past[:, -1:] + jnp.cumsum(logf, axis=1)
    s_past = fox_logits(q, pk, F_new, F_past)
    s_new = fox_logits(q, k, F_new, F_new)
    causal = jnp.tril(jnp.ones((Lq, Lq), dtype=bool))
    s_new = jnp.where(causal, s_new, -jnp.inf)
    p = jax.nn.softmax(jnp.concatenate([s_past, s_new], axis=-1), axis=-1).astype(v.dtype)
    o = (jnp.einsum('bhqk,bkhd->bqhd', p[..., :P], pv)
         + jnp.einsum('bhqk,bkhd->bqhd', p[..., P:], v))
    return o.reshape(Bd, Lq, D_MODEL) @ wo, k, v, logf


def pool_mix(z, pos0, n_out, w, scale):
    B, L, D = z.shape
    zf = z.astype(jnp.float32)
    cs = jnp.concatenate([jnp.zeros((B, 1, D), jnp.float32), jnp.cumsum(zf, axis=1)], axis=1)
    idx = jnp.arange(L - n_out, L)
    pos = pos0 + idx
    parts = []
    for g, win in enumerate(POOL_WINDOWS):
        sl = slice(g * POOL_GROUP_DIM, (g + 1) * POOL_GROUP_DIM)
        lo = jnp.maximum(idx + 1 - win, 0)
        cnt = jnp.minimum(win, pos + 1).astype(jnp.float32)
        parts.append((cs[:, idx + 1, sl] - cs[:, lo, sl]) / cnt[None, :, None])
    d = (jnp.concatenate(parts, axis=-1) - zf[:, L - n_out:]).astype(z.dtype)
    d = d.reshape(B, n_out, POOL_GROUPS, POOL_GROUP_DIM)
    y = jnp.einsum('blgc,gcd->blgd', d, w).reshape(B, n_out, D)
    return y * scale


def gla_project(h, wq, wk, wv, wa1, wa2, ba):
    B, L, _ = h.shape
    q = (h @ wq).reshape(B, L, GLA_HEADS, GLA_DK) * (GLA_DK ** -0.5)
    k = (h @ wk).reshape(B, L, GLA_HEADS, GLA_DK)
    v = (h @ wv).reshape(B, L, GLA_HEADS, GLA_DV)
    g = jax.nn.log_sigmoid(((h @ wa1) @ wa2 + ba).astype(jnp.float32)) / GLA_TAU
    return q, k, v, g.reshape(B, L, GLA_HEADS, GLA_DK)


def gla_output(o, h, wr, gn, wo):
    B, L, _ = h.shape
    o = rms_norm(o, gn).astype(h.dtype)
    r = jax.nn.silu(h @ wr).reshape(B, L, GLA_HEADS, GLA_DV)
    return (o * r).reshape(B, L, D_MODEL) @ wo


def gla_chunked(q, k, v, g, S0):
    B, L, H, _ = q.shape
    C = min(GLA_CHUNK, L)
    N = L // C

    def to_chunks(a):
        return a.astype(jnp.float32).reshape(B, N, C, H, a.shape[-1]).transpose(1, 0, 3, 2, 4)

    qc, kc, vc, gc = to_chunks(q), to_chunks(k), to_chunks(v), to_chunks(g)
    b = jnp.cumsum(gc, axis=3)
    b_last = b[..., -1:, :]
    qe = qc * jnp.exp(b)
    ke = kc * jnp.exp(-b)
    kd = kc * jnp.exp(b_last - b)
    A = jnp.einsum('nbhtk,nbhsk->nbhts', qe, ke)
    A = jnp.where(jnp.tril(jnp.ones((C, C), dtype=bool)), A, 0.0)
    o_intra = jnp.einsum('nbhts,nbhsv->nbhtv', A, vc)

    def step(S, xs):
        qe_c, kd_c, v_c, dl = xs
        o_inter = jnp.einsum('bhtk,bhkv->bhtv', qe_c, S)
        S = S * jnp.exp(dl)[..., :, None] + jnp.einsum('bhsk,bhsv->bhkv', kd_c, v_c)
        return S, o_inter

    S_fin, o_inter = lax.scan(step, S0.astype(jnp.float32), (qe, kd, vc, b_last[..., 0, :]))
    o = (o_intra + o_inter).transpose(1, 0, 3, 2, 4).reshape(B, L, H, GLA_DV)
    return o, S_fin


def gla_recurrent(q, k, v, g, S0):
    xs = tuple(a.astype(jnp.float32).transpose(1, 0, 2, 3) for a in (q, k, v, g))

    def step(S, inp):
        qt, kt, vt, gt = inp
        S = S * jnp.exp(gt)[..., None] + kt[..., :, None] * vt[..., None, :]
        return S, jnp.einsum('bhk,bhkv->bhv', qt, S)

    S_fin, o = lax.scan(step, S0.astype(jnp.float32), xs)
    return o.transpose(1, 0, 2, 3), S_fin


def setup_inputs(seed: int = 0) -> dict:
    key = jax.random.key(seed)
    ks = iter(jax.random.split(key, 48))

    def nrm(shape, scale=1.0):
        return jax.random.normal(next(ks), shape, jnp.float32) * scale

    D = D_MODEL
    n_pages = PAST_LEN // PAGE_SIZE
    n_phys = (5 * DEC_BATCH * n_pages + 3) // 4
    perm = jax.random.permutation(next(ks), n_phys)
    page_table = perm[:DEC_BATCH * n_pages].reshape(DEC_BATCH, n_pages).astype(jnp.int32)
    kv_shape = (N_FOX, n_phys, PAGE_SIZE, FOX_HEADS, FOX_HEAD_DIM)
    sd = D ** -0.5
    return {
        'x_prompt': nrm((BATCH, SEQ, D)),
        'x_sample': nrm((DEC_BATCH, DEC_SEQ, D)),
        'cache_k': nrm(kv_shape),
        'cache_v': nrm(kv_shape),
        'cache_logf': jax.nn.log_sigmoid(2.0 + 0.5 * nrm((N_FOX, n_phys, PAGE_SIZE, FOX_HEADS))),
        'state_pool': nrm((N_POOL_LAYERS, DEC_BATCH, POOL_STATE, D)),
        'state_gla': nrm((N_GLA, DEC_BATCH, GLA_HEADS, GLA_DK, GLA_DV), 0.5),
        'page_table': page_table,
        'c_prompt': nrm((BATCH, D)),
        'c_sample': nrm((DEC_BATCH, D)),
        'norm1': 1.0 + 0.1 * nrm((DEPTH, D)),
        'norm2': 1.0 + 0.1 * nrm((DEPTH, D)),
        'ada_w': nrm((DEPTH, D, 6 * D), 0.5 * sd),
        'ada_b': nrm((DEPTH, 6 * D), 0.02),
        'fox_wq': nrm((N_FOX, D, D), sd),
        'fox_wk': nrm((N_FOX, D, D), sd),
        'fox_wv': nrm((N_FOX, D, D), sd),
        'fox_wf': nrm((N_FOX, D, FOX_HEADS), sd),
        'fox_bf': 2.0 + 0.5 * nrm((N_FOX, FOX_HEADS)),
        'fox_qn': 1.0 + 0.1 * nrm((N_FOX, FOX_HEAD_DIM)),
        'fox_kn': 1.0 + 0.1 * nrm((N_FOX, FOX_HEAD_DIM)),
        'fox_wo': nrm((N_FOX, D, D), sd),
        'pool_w': nrm((N_POOL_LAYERS, POOL_GROUPS, POOL_GROUP_DIM, POOL_GROUP_DIM), POOL_GROUP_DIM ** -0.5),
        'pool_scale': 1.0 + 0.1 * nrm((N_POOL_LAYERS, D)),
        'gla_wq': nrm((N_GLA, D, GLA_HEADS * GLA_DK), sd),
        'gla_wk': nrm((N_GLA, D, GLA_HEADS * GLA_DK), sd),
        'gla_wv': nrm((N_GLA, D, GLA_HEADS * GLA_DV), sd),
        'gla_wa1': nrm((N_GLA, D, GLA_RANK), sd),
        'gla_wa2': nrm((N_GLA, GLA_RANK, GLA_HEADS * GLA_DK), GLA_RANK ** -0.5),
        'gla_ba': nrm((N_GLA, GLA_HEADS * GLA_DK), 0.1),
        'gla_wr': nrm((N_GLA, D, D), sd),
        'gla_gn': 1.0 + 0.1 * nrm((N_GLA, GLA_DV)),
        'gla_wo': nrm((N_GLA, D, D), sd),
        'ffn_wg': nrm((DEPTH, D, D_FF), sd),
        'ffn_wu': nrm((DEPTH, D, D_FF), sd),
        'ffn_wd': nrm((DEPTH, D_FF, D), D_FF ** -0.5),
    }


def reference(x_prompt, x_sample, cache_k, cache_v, cache_logf, state_pool, state_gla, page_table,
              c_prompt, c_sample, norm1, norm2, ada_w, ada_b,
              fox_wq, fox_wk, fox_wv, fox_wf, fox_bf, fox_qn, fox_kn, fox_wo,
              pool_w, pool_scale,
              gla_wq, gla_wk, gla_wv, gla_wa1, gla_wa2, gla_ba, gla_wr, gla_gn, gla_wo,
              ffn_wg, ffn_wu, ffn_wd):
    xp, xs = x_prompt, x_sample
    Lq = x_sample.shape[1]
    past = page_table.shape[1] * PAGE_SIZE
    k_p, v_p, lf_p, k_s, v_s, lf_s = [], [], [], [], [], []
    pool_p, pool_s, gla_p, gla_s = [], [], [], []
    for i in range(DEPTH):
        kind, j = i % N_MIXERS, i // N_MIXERS
        mp_ = ada_split(c_prompt, ada_w[i], ada_b[i])
        ms_ = ada_split(c_sample, ada_w[i], ada_b[i])
        hp = pre_norm(xp, norm1[i], mp_[0], mp_[1])
        hs = pre_norm(xs, norm1[i], ms_[0], ms_[1])
        if kind == 0:
            fw = (fox_wq[j], fox_wk[j], fox_wv[j], fox_wf[j], fox_bf[j], fox_qn[j], fox_kn[j], fox_wo[j])
            op, kp_, vp_, lfp_ = fox_prompt(hp, *fw)
            os_, ks_, vs_, lfs_ = fox_sample(hs, cache_k, cache_v, cache_logf, j, page_table, *fw)
            k_p.append(kp_); v_p.append(vp_); lf_p.append(lfp_)
            k_s.append(ks_); v_s.append(vs_); lf_s.append(lfs_)
        elif kind == 1:
            op = pool_mix(hp, 0, hp.shape[1], pool_w[j], pool_scale[j])
            ext = jnp.concatenate([state_pool[j].astype(hs.dtype), hs], axis=1)
            os_ = pool_mix(ext, past - POOL_STATE, Lq, pool_w[j], pool_scale[j])
            pool_p.append(hp[:, -POOL_STATE:]); pool_s.append(ext[:, -POOL_STATE:])
        else:
            gp = (gla_wq[j], gla_wk[j], gla_wv[j], gla_wa1[j], gla_wa2[j], gla_ba[j])
            q, k, v, g = gla_project(hp, *gp)
            S0 = jnp.zeros((hp.shape[0], GLA_HEADS, GLA_DK, GLA_DV), jnp.float32)
            o, Sp = gla_chunked(q, k, v, g, S0)
            op = gla_output(o, hp, gla_wr[j], gla_gn[j], gla_wo[j])
            q, k, v, g = gla_project(hs, *gp)
            o, Ss = gla_recurrent(q, k, v, g, state_gla[j])
            os_ = gla_output(o, hs, gla_wr[j], gla_gn[j], gla_wo[j])
            gla_p.append(Sp); gla_s.append(Ss)
        xp = xp + mp_[2][:, None, :] * op
        xs = xs + ms_[2][:, None, :] * os_
        fp = pre_norm(xp, norm2[i], mp_[3], mp_[4])
        fs = pre_norm(xs, norm2[i], ms_[3], ms_[4])
        xp = xp + mp_[5][:, None, :] * swiglu(fp, ffn_wg[i], ffn_wu[i], ffn_wd[i])
        xs = xs + ms_[5][:, None, :] * swiglu(fs, ffn_wg[i], ffn_wu[i], ffn_wd[i])
    return (xp, xs,
            jnp.stack(k_p), jnp.stack(v_p), jnp.stack(lf_p),
            jnp.stack(k_s), jnp.stack(v_s), jnp.stack(lf_s),
            jnp.stack(pool_p), jnp.stack(pool_s),
            jnp.stack(gla_p), jnp.stack(gla_s))
```

```python
import functools
import math

import jax
import jax.numpy as jnp
import numpy as np
from jax import lax
from jax.experimental import pallas as pl
from jax.experimental.pallas import tpu as pltpu

F32, BF16 = jnp.float32, jnp.bfloat16

D_MODEL = 1024
N_MIXERS = 3
PAGE_SIZE = 128
FOX_HEADS = 16
FOX_HEAD_DIM = 64
FOX_SCALE = FOX_HEAD_DIM ** -0.5
FOX_AUG_ROWS = 16
POOL_WINDOWS = (2, 4, 8, 16)
POOL_GROUP_DIM = D_MODEL // len(POOL_WINDOWS)
POOL_STATE = max(POOL_WINDOWS) - 1
POOL_HALO = 16
GLA_HEADS = 4
GLA_DK = 128
GLA_DV = 256
GLA_TAU = 16.0
GLA_CHUNK = 64
D_FF = 2816
EPS = 1e-6

V7X_LANES = 128
V7X_VMEM_LIMIT_BYTES = 56 * 1024 * 1024

ROW_BLOCK = 512
POOL_ROW_BLOCK = 256
ATTN_BLOCK = 512
FF_CHUNK = 1408
ADA_COLS = 1536

NT_DIMS = (((1,), (1,)), ((), ()))
TN_DIMS = (((0,), (0,)), ((), ()))


def _params(*sem):
    return pltpu.CompilerParams(dimension_semantics=sem, vmem_limit_bytes=V7X_VMEM_LIMIT_BYTES)


def _resident(shape):
    zeros = (0,) * len(shape)
    return pl.BlockSpec(shape, lambda *_: zeros, pipeline_mode=pl.Buffered(1))


def _silu(x):
    return x * jax.nn.sigmoid(x)


def _log_sigmoid(x):
    return jnp.minimum(x, 0.0) - jnp.log1p(jnp.exp(-jnp.abs(x)))


def _rms(x, g):
    return x * lax.rsqrt(jnp.mean(x * x, axis=-1, keepdims=True) + EPS) * g


def _prenorm(x, g, shift, scale):
    return _rms(x, g) * (1.0 + scale) + shift


def _split3(x):
    hi = x.astype(BF16)
    r1 = x - hi.astype(F32)
    mid = r1.astype(BF16)
    lo = (r1 - mid.astype(F32)).astype(BF16)
    return hi, mid, lo


def _dot3(x, m01):
    hi, mid, lo = _split3(x)
    return (jnp.dot(hi, m01, preferred_element_type=F32)
            + jnp.dot(mid, m01, preferred_element_type=F32)
            + jnp.dot(lo, m01, preferred_element_type=F32))


def _dot3_left(m01, x):
    hi, mid, lo = _split3(x)
    return (jnp.dot(m01, hi, preferred_element_type=F32)
            + jnp.dot(m01, mid, preferred_element_type=F32)
            + jnp.dot(m01, lo, preferred_element_type=F32))


def _ffn_tail(x1, n2, shift2, scale2, gate2, wg_ref, wu_ref, wd_ref):
    f = _prenorm(x1, n2, shift2, scale2).astype(BF16)
    acc = jnp.zeros(x1.shape, F32)
    for c0 in range(0, D_FF, FF_CHUNK):
        g = jnp.dot(f, wg_ref[:, c0:c0 + FF_CHUNK], preferred_element_type=F32)
        u = jnp.dot(f, wu_ref[:, c0:c0 + FF_CHUNK], preferred_element_type=F32)
        a = (_silu(g) * u).astype(BF16)
        acc = acc + jnp.dot(a, wd_ref[c0:c0 + FF_CHUNK, :], preferred_element_type=F32)
    return x1 + gate2 * acc


def _ada_body(c_ref, w_ref, b_ref, o_ref):
    a = _silu(c_ref[...]).astype(BF16)
    o_ref[...] = jnp.dot(a, w_ref[...].astype(BF16), preferred_element_type=F32) + b_ref[...]


def _ada_call(c_all, ada_w, ada_b):
    depth, d, n = ada_w.shape
    rows = c_all.shape[0]
    return pl.pallas_call(
        _ada_body,
        grid=(depth, n // ADA_COLS),
        in_specs=[
            pl.BlockSpec((rows, d), lambda i, j: (0, 0)),
            pl.BlockSpec((None, d, ADA_COLS), lambda i, j: (i, 0, j)),
            pl.BlockSpec((None, 1, ADA_COLS), lambda i, j: (i, 0, j)),
        ],
        out_specs=pl.BlockSpec((None, rows, ADA_COLS), lambda i, j: (i, 0, j)),
        out_shape=jax.ShapeDtypeStruct((depth, rows, n), F32),
        compiler_params=_params("arbitrary", "arbitrary"),
        name="ada_mods",
    )(c_all, ada_w, ada_b.reshape(depth, 1, n))


class _Rows:
    def __init__(self, mods, batch, length, tm, per_row=False):
        self.batch, self.length, self.tm = batch, length, tm
        self.grid = (batch, length // tm)
        self.per_row = per_row
        self.mods = mods if per_row else mods.reshape(batch, 6, 1, D_MODEL)

    def rows(self, cols):
        return pl.BlockSpec((None, self.tm, cols), lambda b, i: (b, i, 0))

    def mod(self, c):
        if self.per_row:
            return pl.BlockSpec((self.tm, D_MODEL), lambda b, i: (i, c))
        return pl.BlockSpec((None, None, 1, D_MODEL), lambda b, i: (b, c, 0, 0))

    def shape(self, cols, dtype=F32):
        return jax.ShapeDtypeStruct((self.batch, self.length, cols), dtype)


def _head_norm(y, gain_col):
    ms = jnp.mean(y * y, axis=1, keepdims=True)
    return y * lax.rsqrt(ms + EPS) * gain_col[None, :, :]


def _fox_proj_body(x_ref, n1_ref, shift_ref, scale_ref, wq_ref, wk_ref, wv_ref, wf_ref, bf_ref,
                   qn_ref, kn_ref, tri_ref, *out_refs, with_bias_rows):
    tm = x_ref.shape[0]
    hb = _prenorm(x_ref[...], n1_ref[...], shift_ref[...], scale_ref[...]).astype(BF16)

    def proj_t(w_ref):
        return lax.dot_general(w_ref[...], hb, NT_DIMS, preferred_element_type=F32)

    q = _head_norm(proj_t(wq_ref).reshape(FOX_HEADS, FOX_HEAD_DIM, tm), qn_ref[...])
    k = _head_norm(proj_t(wk_ref).reshape(FOX_HEADS, FOX_HEAD_DIM, tm), kn_ref[...])
    v = proj_t(wv_ref).reshape(FOX_HEADS, FOX_HEAD_DIM, tm)
    lf = _log_sigmoid(proj_t(wf_ref) + bf_ref[...])

    if not with_bias_rows:
        q_ref, k_ref, v_ref, lf_ref = out_refs
        q_ref[...], k_ref[...], v_ref[...], lf_ref[...] = q, k, v, lf
        return

    qa_ref, ka_ref, k_ref, v_ref, lf_ref, carry_ref = out_refs
    k_ref[...], v_ref[...], lf_ref[...] = k, v, lf

    @pl.when(pl.program_id(1) == 0)
    def _():
        carry_ref[...] = jnp.zeros_like(carry_ref)

    f_cum = _dot3(lf, tri_ref[...]) + carry_ref[:, 0:1]
    carry_ref[...] = jnp.broadcast_to(f_cum[:, tm - 1:tm], carry_ref.shape)

    hi, mid, lo = (p.astype(F32)[:, None, :] for p in _split3(f_cum))
    r = lax.broadcasted_iota(jnp.int32, (FOX_HEADS, FOX_AUG_ROWS, tm), 1)
    q_rows = jnp.where(r == 0, hi, jnp.where(r == 1, mid, jnp.where(r == 2, lo, jnp.where(r < 6, 1.0, 0.0))))
    k_rows = jnp.where(r < 3, 1.0, jnp.where(r == 3, -hi, jnp.where(r == 4, -mid, jnp.where(r == 5, -lo, 0.0))))
    pad =jnp.zeros((FOX_HEADS, 2 * FOX_HEAD_DIM - FOX_HEAD_DIM - FOX_AUG_ROWS, tm), BF16)
    qa_ref[:, 0:FOX_HEAD_DIM, :] = (q * FOX_SCALE).astype(BF16)
    qa_ref[:, FOX_HEAD_DIM:FOX_HEAD_DIM + FOX_AUG_ROWS, :] = q_rows.astype(BF16)
    qa_ref[:, FOX_HEAD_DIM + FOX_AUG_ROWS:, :] = pad
    ka_ref[:, 0:FOX_HEAD_DIM, :] = k.astype(BF16)
    ka_ref[:, FOX_HEAD_DIM:FOX_HEAD_DIM + FOX_AUG_ROWS, :] = k_rows.astype(BF16)
    ka_ref[:, FOX_HEAD_DIM + FOX_AUG_ROWS:, :] = pad


def _fox_proj_call(x, n1, rs, w, with_bias_rows):
    batch, length, tm = rs.batch, rs.length, rs.tm
    hshape = lambda rows, dt: jax.ShapeDtypeStruct((batch, FOX_HEADS, rows, length), dt)
    hspec = lambda rows: pl.BlockSpec((None, FOX_HEADS, rows, tm), lambda b, i: (b, 0, 0, i))
    lf_shape = jax.ShapeDtypeStruct((batch, FOX_HEADS, length), F32)
    lf_spec = pl.BlockSpec((None, FOX_HEADS, tm), lambda b, i: (b, 0, i))
    if with_bias_rows:
        out_shape = (hshape(2 * FOX_HEAD_DIM, BF16), hshape(2 * FOX_HEAD_DIM, BF16),
                     hshape(FOX_HEAD_DIM, F32), hshape(FOX_HEAD_DIM, F32), lf_shape)
        out_specs = (hspec(2 * FOX_HEAD_DIM), hspec(2 * FOX_HEAD_DIM), hspec(FOX_HEAD_DIM), hspec(FOX_HEAD_DIM), lf_spec)
        scratch = [pltpu.VMEM((FOX_HEADS, V7X_LANES), F32)]
    else:
        out_shape = (hshape(FOX_HEAD_DIM, F32),) * 3 + (lf_shape,)
        out_specs = (hspec(FOX_HEAD_DIM),) * 3 + (lf_spec,)
        scratch = []
    tri = jnp.triu(jnp.ones((tm, tm), BF16))
    return pl.pallas_call(
        functools.partial(_fox_proj_body, with_bias_rows=with_bias_rows),
        grid=rs.grid,
        in_specs=[rs.rows(D_MODEL), _resident((1, D_MODEL)), rs.mod(0), rs.mod(1),
                  _resident((D_MODEL, D_MODEL)), _resident((D_MODEL, D_MODEL)), _resident((D_MODEL, D_MODEL)),
                  _resident((FOX_HEADS, D_MODEL)), _resident((FOX_HEADS, 1)),
                  _resident((FOX_HEAD_DIM, 1)), _resident((FOX_HEAD_DIM, 1)), _resident((tm, tm))],
        out_specs=out_specs,
        out_shape=out_shape,
        scratch_shapes=scratch,
        compiler_params=_params("arbitrary", "arbitrary"),
        name="fox_proj_bias" if with_bias_rows else "fox_proj",
    )(x, n1, rs.mods, rs.mods, w["wq_t"], w["wk_t"], w["wv_t"], w["wf_t"], w["bf"], w["qn"], w["kn"], tri)


def _fox_attn_body(qi_tab, ki_tab, q_ref, k_ref, v_ref, o_ref, qs_ref, m_ref, l_ref, acc_ref):
    t = pl.program_id(2)
    qi, ki = qi_tab[t], ki_tab[t]
    tq, tk = qs_ref.shape[1], k_ref.shape[2]

    @pl.when(ki == 0)
    def _():
        for hh in range(2):
            qs_ref[hh] = q_ref[hh].T
        m_ref[...] = jnp.full_like(m_ref, -jnp.inf)
        l_ref[...] = jnp.zeros_like(l_ref)
        acc_ref[...] = jnp.zeros_like(acc_ref)

    vb = v_ref[...].astype(BF16)
    row = lax.broadcasted_iota(jnp.int32, (tq, tk), 0)
    col = lax.broadcasted_iota(jnp.int32, (tq, tk), 1)
    hidden = jnp.logical_and(col > row, ki == qi)
    alphas, pvs = [], []
    for hh in range(2):
        s = jnp.dot(qs_ref[hh], k_ref[hh], preferred_element_type=F32)
        s = jnp.where(hidden, -jnp.inf, s)
        m_prev = m_ref[hh]
        m_new = jnp.maximum(m_prev, jnp.max(s, axis=-1, keepdims=True))
        alpha = jnp.exp(m_prev - m_new)
        p = jnp.exp(s - m_new[:, 0:1])
        l_ref[hh] = alpha * l_ref[hh] + jnp.sum(p, axis=-1, keepdims=True)
        m_ref[hh] = m_new
        alphas.append(alpha)
        pvs.append(lax.dot_general(p.astype(BF16), vb, NT_DIMS, preferred_element_type=F32))
    first = lax.broadcasted_iota(jnp.int32, (tq, 2 * FOX_HEAD_DIM), 1) < FOX_HEAD_DIM
    acc_ref[...] = acc_ref[...] * jnp.where(first, alphas[0], alphas[1]) + jnp.where(first, pvs[0], pvs[1])

    @pl.when(ki == qi)
    def _():
        o_ref[...] = (acc_ref[...] / jnp.where(first, l_ref[0], l_ref[1])).astype(o_ref.dtype)


def _fox_attn_call(qa, ka, v_t):
    batch, _, _, length = qa.shape
    tq = tk = ATTN_BLOCK
    nq = length // tq
    pairs = [(i, j) for i in range(nq) for j in range(i + 1)]
    qi_tab = jnp.asarray([p[0] for p in pairs], jnp.int32)
    ki_tab = jnp.asarray([p[1] for p in pairs], jnp.int32)
    pair_rows = 2 * FOX_HEAD_DIM
    qa = qa.reshape(batch, FOX_HEADS // 2, 2, pair_rows, length)
    ka = ka.reshape(batch, FOX_HEADS // 2, 2, pair_rows, length)
    v_t = v_t.reshape(batch, FOX_HEADS // 2, pair_rows, length)
    return pl.pallas_call(
        _fox_attn_body,
        grid_spec=pltpu.PrefetchScalarGridSpec(
            num_scalar_prefetch=2,
            grid=(batch, FOX_HEADS // 2, len(pairs)),
            in_specs=[
                pl.BlockSpec((None, None, 2, pair_rows, tq), lambda b, h, t, qt, kt: (b, h, 0, 0, qt[t])),
                pl.BlockSpec((None, None, 2, pair_rows, tk), lambda b, h, t, qt, kt: (b, h, 0, 0, kt[t])),
                pl.BlockSpec((None, None, pair_rows, tk), lambda b, h, t, qt, kt: (b, h, 0, kt[t])),
            ],
            out_specs=pl.BlockSpec((None, tq, pair_rows), lambda b, h, t, qt, kt: (b, qt[t], h)),
            scratch_shapes=[pltpu.VMEM((2, tq, pair_rows), BF16), pltpu.VMEM((2, tq, V7X_LANES), F32),
                            pltpu.VMEM((2, tq, V7X_LANES), F32), pltpu.VMEM((tq, pair_rows), F32)],
        ),
        out_shape=jax.ShapeDtypeStruct((batch, length, D_MODEL), BF16),
        compiler_params=_params("arbitrary", "arbitrary", "arbitrary"),
        name="fox_attn",
    )(qi_tab, ki_tab, qa, ka, v_t)


def _lane_column(x_t, sel):
    return _dot3(x_t, sel)


def _fox_decode_body(pt_ref, ck_ref, cv_ref, clf_ref, q_ref, k_ref, v_ref, lf_ref, o_ref,
                     qcol_ref, m_ref, l_ref, c_ref, acc_ref, ot_ref):
    b, p = pl.program_id(0), pl.program_id(1)
    n_pages = pl.num_programs(1)
    hs = (FOX_HEADS, FOX_HEAD_DIM, PAGE_SIZE)
    lane = lax.broadcasted_iota(jnp.int32, (FOX_HEADS, PAGE_SIZE), 1)

    @pl.when(jnp.logical_and(b == 0, p == 0))
    def _():
        ot_ref[...] = jnp.zeros_like(ot_ref)

    @pl.when(p == 0)
    def _():
        sel = (lax.broadcasted_iota(jnp.int32, (PAGE_SIZE, PAGE_SIZE), 0) == b).astype(BF16)
        rows = FOX_HEADS * FOX_HEAD_DIM
        qcol = _lane_column(q_ref[...].reshape(rows, PAGE_SIZE), sel).reshape(hs)
        kcol = _lane_column(k_ref[...].reshape(rows, PAGE_SIZE), sel).reshape(hs)
        vcol = _lane_column(v_ref[...].reshape(rows, PAGE_SIZE), sel).reshape(hs)
        qcol_ref[...] = qcol
        m_ref[...] = jnp.sum(qcol * kcol, axis=1) * FOX_SCALE
        l_ref[...] = jnp.where(lane == 0, 1.0, 0.0)
        acc_ref[...] = jnp.where(lane[:, None, :] == 0, vcol, 0.0)
        c_ref[...] = _lane_column(lf_ref[...], sel)

    lf_page = clf_ref[...]
    after = (lax.broadcasted_iota(jnp.int32, (PAGE_SIZE, PAGE_SIZE), 0)
             > lax.broadcasted_iota(jnp.int32, (PAGE_SIZE, PAGE_SIZE), 1)).astype(BF16)
    suffix = _dot3(lf_page, after)
    bias = c_ref[...] + suffix
    c_ref[...] = c_ref[...] + jnp.broadcast_to(suffix[:, 0:1] + lf_page[:, 0:1], lane.shape)

    s = jnp.sum(qcol_ref[...] * ck_ref[...], axis=1) * FOX_SCALE + bias
    m_prev = m_ref[...]
    m_new = jnp.maximum(m_prev, jnp.max(s, axis=-1, keepdims=True))
    alpha = jnp.exp(m_prev - m_new)
    pr = jnp.exp(s - m_new)
    m_ref[...] = m_new
    l_ref[...] = alpha * l_ref[...] + pr
    acc_ref[...] = acc_ref[...] * alpha[:, None, :] + cv_ref[...] * pr[:, None, :]

    @pl.when(p == n_pages - 1)
    def _():
        num = jnp.sum(acc_ref[...], axis=-1, keepdims=True)
        den = jnp.sum(l_ref[...], axis=-1, keepdims=True)[:, None, :]
        o = jnp.broadcast_to(num / den, hs)
        ot_ref[...] = jnp.where(lane[:, None, :] == b, o, ot_ref[...])

    @pl.when(jnp.logical_and(b == pl.num_programs(0) - 1, p == n_pages - 1))
    def _():
        o_ref[...] = ot_ref[...].reshape(FOX_HEADS * FOX_HEAD_DIM, PAGE_SIZE).T.astype(o_ref.dtype)


def _fox_decode_call(page_table, ck_t, cv_t, clf_t, j, q_t, k_t, v_t, lf_t):
    n_seq, n_pages = page_table.shape
    assert n_seq == PAGE_SIZE, "the sample batch is laid out on the 128 lanes"
    hs = (FOX_HEADS, FOX_HEAD_DIM, PAGE_SIZE)
    page = lambda b, p, pt: (j, pt[b, n_pages - 1 - p], 0, 0, 0)
    return pl.pallas_call(
        _fox_decode_body,
        grid_spec=pltpu.PrefetchScalarGridSpec(
            num_scalar_prefetch=1,
            grid=(n_seq, n_pages),
            in_specs=[
                pl.BlockSpec((None, None) + hs, page),
                pl.BlockSpec((None, None) + hs, page),
                pl.BlockSpec((None, None, FOX_HEADS, PAGE_SIZE), lambda b, p, pt: (j, pt[b, n_pages - 1 - p], 0, 0)),
                pl.BlockSpec(hs, lambda b, p, pt: (0, 0, 0)),
                pl.BlockSpec(hs, lambda b, p, pt: (0, 0, 0)),
                pl.BlockSpec(hs, lambda b, p, pt: (0, 0, 0)),
                pl.BlockSpec((FOX_HEADS, PAGE_SIZE), lambda b, p, pt: (0, 0)),
            ],
            out_specs=pl.BlockSpec((n_seq, D_MODEL), lambda b, p, pt: (0, 0)),
            scratch_shapes=[pltpu.VMEM(hs, F32), pltpu.VMEM((FOX_HEADS, PAGE_SIZE), F32),
                            pltpu.VMEM((FOX_HEADS, PAGE_SIZE), F32), pltpu.VMEM((FOX_HEADS, PAGE_SIZE), F32),
                            pltpu.VMEM(hs, F32), pltpu.VMEM(hs, F32)],
        ),
        out_shape=jax.ShapeDtypeStruct((n_seq, D_MODEL), BF16),
        compiler_params=_params("arbitrary", "arbitrary"),
        name="fox_decode",
    )(page_table, ck_t, cv_t, clf_t, q_t, k_t, v_t, lf_t)


def _out_ffn_body(x_ref, y_ref, wo_ref, gate1_ref, n2_ref, shift2_ref, scale2_ref, gate2_ref,
                  wg_ref, wu_ref, wd_ref, o_ref):
    mix = jnp.dot(y_ref[...], wo_ref[...], preferred_element_type=F32)
    x1 = x_ref[...] + gate1_ref[...] * mix
    o_ref[...] = _ffn_tail(x1, n2_ref[...], shift2_ref[...], scale2_ref[...], gate2_ref[...], wg_ref, wu_ref, wd_ref)


def _ffn_specs():
    return [_resident((D_MODEL, D_FF)), _resident((D_MODEL, D_FF)), _resident((D_FF, D_MODEL))]


def _out_ffn_call(x, y, wo, n2, rs, ffn):
    return pl.pallas_call(
        _out_ffn_body,
        grid=rs.grid,
        in_specs=[rs.rows(D_MODEL), rs.rows(D_MODEL), _resident((D_MODEL, D_MODEL)), rs.mod(2),
                  _resident((1, D_MODEL)), rs.mod(3), rs.mod(4), rs.mod(5)] + _ffn_specs(),
        out_specs=rs.rows(D_MODEL),
        out_shape=rs.shape(D_MODEL),
        compiler_params=_params("arbitrary", "arbitrary"),
        name="out_ffn",
    )(x, y, wo, rs.mods, n2, rs.mods, rs.mods, rs.mods, *ffn)


def _pool_groups(window_sums, h, inv_cnt, pw_ref, ps_ref):
    outs = []
    for g in range(len(POOL_WINDOWS)):
        sl = slice(g * POOL_GROUP_DIM, (g + 1) * POOL_GROUP_DIM)
        d = (window_sums[g] * inv_cnt[g] - h[:, sl]).astype(BF16)
        outs.append(jnp.dot(d, pw_ref[g], preferred_element_type=F32))
    return jnp.concatenate(outs, axis=-1) * ps_ref[...]


def _pool_ffn_body(x_ref, n1_ref, shift1_ref, scale1_ref, gate1_ref, pw_ref, ps_ref,
                   n2_ref, shift2_ref, scale2_ref, gate2_ref, wg_ref, wu_ref, wd_ref,
                   o_ref, tail_ref, ext_ref):
    tm = x_ref.shape[0]
    i = pl.program_id(1)
    x = x_ref[...]
    h = _prenorm(x, n1_ref[...], shift1_ref[...], scale1_ref[...])

    @pl.when(i == 0)
    def _():
        ext_ref[0:POOL_HALO, :] = jnp.zeros((POOL_HALO, D_MODEL), F32)

    ext_ref[POOL_HALO:, :] = h
    pos = i * tm + lax.broadcasted_iota(jnp.int32, (tm, 1), 0)
    sums, inv_cnt = [], []
    run, back = h, 1
    for g, win in enumerate(POOL_WINDOWS):
        lo = g * POOL_GROUP_DIM
        while back < win:
            run = run + ext_ref[POOL_HALO - back:POOL_HALO - back + tm, lo:]
            back += 1
        sums.append(run[:, :POOL_GROUP_DIM])
        if g + 1 < len(POOL_WINDOWS):
            run = run[:, POOL_GROUP_DIM:]
        inv_cnt.append(1.0 / jnp.minimum(win, pos + 1).astype(F32))
    y = _pool_groups(sums, h, inv_cnt, pw_ref, ps_ref)
    x1 = x + gate1_ref[...] * y
    o_ref[...] = _ffn_tail(x1, n2_ref[...], shift2_ref[...], scale2_ref[...], gate2_ref[...], wg_ref, wu_ref, wd_ref)
    halo = ext_ref[tm:tm + POOL_HALO, :]
    ext_ref[0:POOL_HALO, :] = halo
    tail_ref[...] = halo


def _pool_ffn_call(x, n1, pool_w, pool_scale, n2, rs, ffn):
    g = len(POOL_WINDOWS)
    return pl.pallas_call(
        _pool_ffn_body,
        grid=rs.grid,
        in_specs=[rs.rows(D_MODEL), _resident((1, D_MODEL)), rs.mod(0), rs.mod(1), rs.mod(2),
                  _resident((g, POOL_GROUP_DIM, POOL_GROUP_DIM)), _resident((1, D_MODEL)),
                  _resident((1, D_MODEL)), rs.mod(3), rs.mod(4), rs.mod(5)] + _ffn_specs(),
        out_specs=(rs.rows(D_MODEL), pl.BlockSpec((None, POOL_HALO, D_MODEL), lambda b, i: (b, 0, 0))),
        out_shape=(rs.shape(D_MODEL), jax.ShapeDtypeStruct((rs.batch, POOL_HALO, D_MODEL), F32)),
        scratch_shapes=[pltpu.VMEM((POOL_HALO + rs.tm, D_MODEL), F32)],
        compiler_params=_params("arbitrary", "arbitrary"),
        name="pool_ffn",
    )(x, n1, rs.mods, rs.mods, rs.mods, pool_w, pool_scale, n2, rs.mods, rs.mods, rs.mods, *ffn)


def _pool_sample_body(x_ref, st_ref, n1_ref, shift1_ref, scale1_ref, gate1_ref, pw_ref, ps_ref,
                      n2_ref, shift2_ref, scale2_ref, gate2_ref, wg_ref, wu_ref, wd_ref, o_ref, st_out_ref):
    x = x_ref[...]
    h = _prenorm(x, n1_ref[...], shift1_ref[...], scale1_ref[...])
    sums, inv_cnt = [], []
    run = h
    back = 1
    for g, win in enumerate(POOL_WINDOWS):
        lo = g * POOL_GROUP_DIM
        while back < win:
            run = run + st_ref[POOL_STATE - back]
            back += 1
        sums.append(run[:, lo:lo + POOL_GROUP_DIM])
        inv_cnt.append(1.0 / win)
    y = _pool_groups(sums, h, inv_cnt, pw_ref, ps_ref)
    x1 = x + gate1_ref[...] * y
    o_ref[...] = _ffn_tail(x1, n2_ref[...], shift2_ref[...], scale2_ref[...], gate2_ref[...], wg_ref, wu_ref, wd_ref)
    st_out_ref[0:POOL_STATE - 1] = st_ref[1:POOL_STATE]
    st_out_ref[POOL_STATE - 1] = h


def _pool_sample_call(x, state_t, n1, pool_w, pool_scale, n2, mods, ffn):
    bd = x.shape[0]
    g = len(POOL_WINDOWS)
    mod = lambda c: pl.BlockSpec((bd, D_MODEL), lambda i: (0, c), pipeline_mode=pl.Buffered(1))
    return pl.pallas_call(
        _pool_sample_body,
        grid=(1,),
        in_specs=[_resident((bd, D_MODEL)), _resident((POOL_STATE, bd, D_MODEL)), _resident((1, D_MODEL)),
                  mod(0), mod(1), mod(2), _resident((g, POOL_GROUP_DIM, POOL_GROUP_DIM)), _resident((1, D_MODEL)),
                  _resident((1, D_MODEL)), mod(3), mod(4), mod(5)] + _ffn_specs(),
        out_specs=(pl.BlockSpec((bd, D_MODEL), lambda i: (0, 0)),
                   pl.BlockSpec((POOL_STATE, bd, D_MODEL), lambda i: (0, 0, 0))),
        out_shape=(jax.ShapeDtypeStruct((bd, D_MODEL), F32), jax.ShapeDtypeStruct((POOL_STATE, bd, D_MODEL), F32)),
        compiler_params=_params("arbitrary"),
        name="pool_sample",
    )(x, state_t, n1, mods, mods, mods, pool_w, pool_scale, n2, mods, mods, mods, *ffn)


def _gla_gate(hb, wa1_ref, wa2_ref, ba):
    a1 = jnp.dot(hb, wa1_ref[...], preferred_element_type=F32).astype(BF16)
    return _log_sigmoid(jnp.dot(a1, wa2_ref[...], preferred_element_type=F32) + ba) / GLA_TAU


def _gla_proj_body(x_ref, n1_ref, shift_ref, scale_ref, wq_ref, wk_ref, wv_ref, wa1_ref, wa2_ref, ba_ref, wr_ref,
                   q_ref, k_ref, v_ref, g_ref, r_ref):
    hb = _prenorm(x_ref[...], n1_ref[...], shift_ref[...], scale_ref[...]).astype(BF16)
    q_ref[...] = jnp.dot(hb, wq_ref[...], preferred_element_type=F32) * (GLA_DK ** -0.5)
    k_ref[...] = jnp.dot(hb, wk_ref[...], preferred_element_type=F32)
    v_ref[...] = jnp.dot(hb, wv_ref[...], preferred_element_type=F32).astype(v_ref.dtype)
    g_ref[...] = _gla_gate(hb, wa1_ref, wa2_ref, ba_ref[...])
    r_ref[...] = _silu(jnp.dot(hb, wr_ref[...], preferred_element_type=F32))


def _gla_proj_call(x, n1, rs, w):
    kw = GLA_HEADS * GLA_DK
    rank = w["wa1"].shape[1]
    return pl.pallas_call(
        _gla_proj_body,
        grid=rs.grid,
        in_specs=[rs.rows(D_MODEL), _resident((1, D_MODEL)), rs.mod(0), rs.mod(1),
                  _resident((D_MODEL, kw)), _resident((D_MODEL, kw)), _resident((D_MODEL, D_MODEL)),
                  _resident((D_MODEL, rank)), _resident((rank, kw)), _resident((1, kw)), _resident((D_MODEL, D_MODEL))],
        out_specs=(rs.rows(kw), rs.rows(kw), rs.rows(D_MODEL), rs.rows(kw), rs.rows(D_MODEL)),
        out_shape=(rs.shape(kw), rs.shape(kw), rs.shape(D_MODEL, BF16), rs.shape(kw), rs.shape(D_MODEL)),
        compiler_params=_params("arbitrary", "arbitrary"),
        name="gla_proj",
    )(x, n1, rs.mods, rs.mods, w["wq"], w["wk"], w["wv"], w["wa1"], w["wa2"], w["ba"], w["wr"])


def _gla_out_norm(o, gn, r):
    return _rms(o, gn) * r


def _gla_scan_body(q_ref, k_ref, v_ref, g_ref, r_ref, gn_ref, y_ref, s_out_ref, st_ref):
    batch, rows = q_ref.shape[0], q_ref.shape[1]
    c = GLA_CHUNK

    @pl.when(pl.program_id(0) == 0)
    def _():
        st_ref[...] = jnp.zeros_like(st_ref)

    ri = lax.broadcasted_iota(jnp.int32, (c, c), 0)
    ci = lax.broadcasted_iota(jnp.int32, (c, c), 1)
    causal = ci <= ri
    tril = causal.astype(BF16)
    gn = gn_ref[...]

    def chunk(n, carry):
        r0 = pl.multiple_of(n * c, c)
        for b in range(batch):
            q, k, g = (ref[b, pl.ds(r0, c), :] for ref in (q_ref, k_ref, g_ref))
            v, rr = v_ref[b, pl.ds(r0, c), :], r_ref[b, pl.ds(r0, c), :]
            bcum = _dot3_left(tril, g)
            b_last = bcum[c - 1:c, :]
            e = jnp.exp(bcum)
            qe = (q * e).astype(BF16)
            ke = (k * jnp.exp(-bcum)).astype(BF16)
            kd = (k * jnp.exp(b_last - bcum)).astype(BF16)
            decay = jnp.exp(b_last)
            for h in range(GLA_HEADS):
                ks = slice(h * GLA_DK, (h + 1) * GLA_DK)
                vs = slice(h * GLA_DV, (h + 1) * GLA_DV)
                st = st_ref[b, h]
                a = lax.dot_general(qe[:, ks], ke[:, ks], NT_DIMS, preferred_element_type=F32)
                a = jnp.where(causal, a, 0.0).astype(BF16)
                o = (jnp.dot(a, v[:, vs], preferred_element_type=F32)
                     + lax.dot_general(qe[:, ks], st.astype(BF16), NT_DIMS, preferred_element_type=F32))
                st_ref[b, h] = st * decay[:, ks] + lax.dot_general(v[:, vs], kd[:, ks], TN_DIMS,
                                                                  preferred_element_type=F32)
                y_ref[b, pl.ds(r0, c), vs] = _gla_out_norm(o, gn, rr[:, vs]).astype(BF16)
        return carry

    lax.fori_loop(0, rows // c, chunk, 0)

    @pl.when(pl.program_id(0) == pl.num_programs(0) - 1)
    def _():
        for b in range(batch):
            for h in range(GLA_HEADS):
                s_out_ref[b, h] = st_ref[b, h].T


def _gla_scan_call(q, k, v, g, r, gn, rows):
    batch, length, kw = q.shape
    blk = lambda cols: pl.BlockSpec((batch, rows, cols), lambda i: (0, i, 0))
    s_shape = (batch, GLA_HEADS, GLA_DK, GLA_DV)
    return pl.pallas_call(
        _gla_scan_body,
        grid=(length // rows,),
        in_specs=[blk(kw), blk(kw), blk(D_MODEL), blk(kw), blk(D_MODEL), _resident((1, GLA_DV))],
        out_specs=(blk(D_MODEL), pl.BlockSpec(s_shape, lambda i: (0, 0, 0, 0))),
        out_shape=(jax.ShapeDtypeStruct((batch, length, D_MODEL), BF16), jax.ShapeDtypeStruct(s_shape, F32)),
        scratch_shapes=[pltpu.VMEM((batch, GLA_HEADS, GLA_DV, GLA_DK), F32)],
        compiler_params=_params("arbitrary"),
        name="gla_scan",
    )(q, k, v, g, r, gn)


def _gla_sample_proj_body(x_ref, n1_ref, shift_ref, scale_ref, wq_ref, wk_ref, wv_ref, wa1_ref, wa2_ref, ba_ref,
                          wr_ref, q_ref, k_ref, v_ref, g_ref, r_ref):
    hb = _prenorm(x_ref[...], n1_ref[...], shift_ref[...], scale_ref[...]).astype(BF16)
    q_ref[...] = jnp.dot(hb, wq_ref[...], preferred_element_type=F32).T * (GLA_DK ** -0.5)
    k_ref[...] = jnp.dot(hb, wk_ref[...], preferred_element_type=F32).T
    v_ref[...] = jnp.dot(hb, wv_ref[...], preferred_element_type=F32)
    g_ref[...] = _gla_gate(hb, wa1_ref, wa2_ref, ba_ref[...]).T
    r_ref[...] = _silu(jnp.dot(hb, wr_ref[...], preferred_element_type=F32))


def _gla_sample_proj_call(x, n1, mods, w):
    bd = x.shape[0]
    kw = GLA_HEADS * GLA_DK
    rank = w["wa1"].shape[1]
    mod = lambda c: pl.BlockSpec((bd, D_MODEL), lambda i: (0, c), pipeline_mode=pl.Buffered(1))
    full = lambda *s: pl.BlockSpec(s, lambda i: (0,) * len(s))
    return pl.pallas_call(
        _gla_sample_proj_body,
        grid=(1,),
        in_specs=[_resident((bd, D_MODEL)), _resident((1, D_MODEL)), mod(0), mod(1),
                  _resident((D_MODEL, kw)), _resident((D_MODEL, kw)), _resident((D_MODEL, D_MODEL)),
                  _resident((D_MODEL, rank)), _resident((rank, kw)), _resident((1, kw)), _resident((D_MODEL, D_MODEL))],
        out_specs=(full(kw, bd), full(kw, bd), full(bd, D_MODEL), full(kw, bd), full(bd, D_MODEL)),
        out_shape=(jax.ShapeDtypeStruct((kw, bd), F32), jax.ShapeDtypeStruct((kw, bd), F32),
                   jax.ShapeDtypeStruct((bd, D_MODEL), F32), jax.ShapeDtypeStruct((kw, bd), F32),
                   jax.ShapeDtypeStruct((bd, D_MODEL), F32)),
        compiler_params=_params("arbitrary"),
        name="gla_sample_proj",
    )(x, n1, mods, mods, w["wq"], w["wk"], w["wv"], w["wa1"], w["wa2"], w["ba"], w["wr"])


def _gla_decode_body(q_ref, k_ref, g_ref, v_ref, r_ref, gn_ref, s_ref, y_ref, s_out_ref):
    b = pl.program_id(0)
    n = q_ref.shape[1]
    sel = (lax.broadcasted_iota(jnp.int32, (n, n), 0) == b).astype(BF16)
    wide = lambda col: jnp.concatenate([col, col], axis=-1).reshape(GLA_HEADS, GLA_DK, GLA_DV)
    qcol = wide(_lane_column(q_ref[...], sel))
    kcol = wide(_lane_column(k_ref[...], sel))
    decay = wide(jnp.exp(_lane_column(g_ref[...], sel)))
    gn = gn_ref[...]
    for h in range(GLA_HEADS):
        vs = slice(h * GLA_DV, (h + 1) * GLA_DV)
        s_new = s_ref[h] * decay[h] + kcol[h] * v_ref[:, vs]
        s_out_ref[h] = s_new
        o = jnp.sum(qcol[h] * s_new, axis=0, keepdims=True)
        y_ref[:, vs] = _gla_out_norm(o, gn, r_ref[:, vs])


def _gla_decode_call(q_t, k_t, g_t, v, r, gn, state):
    kw, bd = q_t.shape
    assert bd == V7X_LANES, "the sample batch is laid out on the 128 lanes"
    full = lambda *s: pl.BlockSpec(s, lambda i: (0,) * len(s))
    row = pl.BlockSpec((None, 1, D_MODEL), lambda i: (i, 0, 0))
    st = pl.BlockSpec((None, GLA_HEADS, GLA_DK, GLA_DV), lambda i: (i, 0, 0, 0))
    return pl.pallas_call(
        _gla_decode_body,
        grid=(bd,),
        in_specs=[full(kw, bd), full(kw, bd), full(kw, bd), row, row, full(1, GLA_DV), st],
        out_specs=(row, st),
        out_shape=(jax.ShapeDtypeStruct((bd, 1, D_MODEL), F32), jax.ShapeDtypeStruct(state.shape, F32)),
        compiler_params=_params("arbitrary"),
        name="gla_decode",
    )(q_t, k_t, g_t, v, r, gn, state)


def kernel(x_prompt, x_sample, cache_k, cache_v, cache_logf, state_pool, state_gla, page_table, c_prompt, c_sample, norm1, norm2, ada_w, ada_b, fox_wq, fox_wk, fox_wv, fox_wf, fox_bf, fox_qn, fox_kn, fox_wo, pool_w, pool_scale, gla_wq, gla_wk, gla_wv, gla_wa1, gla_wa2, gla_ba, gla_wr, gla_gn, gla_wo, ffn_wg, ffn_wu, ffn_wd):
    batch, length, d = x_prompt.shape
    bd = x_sample.shape[0]
    depth = norm1.shape[0]
    assert d == D_MODEL and x_sample.shape[1] == 1

    c_all = jnp.concatenate([c_prompt, c_sample], axis=0)
    c_all = jnp.pad(c_all, ((0, -c_all.shape[0] % 8), (0, 0)))
    mods = _ada_call(c_all, ada_w, ada_b)

    ck_t = jnp.transpose(cache_k, (0, 1, 3, 4, 2))
    cv_t = jnp.transpose(cache_v, (0, 1, 3, 4, 2))
    clf_t = jnp.transpose(cache_logf, (0, 1, 3, 2))

    xp = x_prompt
    xs = x_sample.reshape(1, bd, d)
    outs = {k: [] for k in ("k_p", "v_p", "lf_p", "k_s", "v_s", "lf_s", "pool_p", "pool_s", "gla_p", "gla_s")}
    for i in range(depth):
        kind, j = i % N_MIXERS, i // N_MIXERS
        rp = _Rows(mods[i, :batch], batch, length, ROW_BLOCK)
        rsm = _Rows(mods[i, batch:batch + bd], 1, bd, bd, per_row=True)
        n1, n2 = norm1[i][None], norm2[i][None]
        ffn = (ffn_wg[i].astype(BF16), ffn_wu[i].astype(BF16), ffn_wd[i].astype(BF16))
        if kind == 0:
            w = dict(wq_t=fox_wq[j].T.astype(BF16), wk_t=fox_wk[j].T.astype(BF16), wv_t=fox_wv[j].T.astype(BF16),
                     wf_t=fox_wf[j].T.astype(BF16), bf=fox_bf[j][:, None], qn=fox_qn[j][:, None], kn=fox_kn[j][:, None])
            wo = fox_wo[j].astype(BF16)
            qa, ka, k_t, v_t, lf_t = _fox_proj_call(xp, n1, rp, w, True)
            yp = _fox_attn_call(qa, ka, v_t)
            qs_t, ks_t, vs_t, lfs_t = _fox_proj_call(xs, n1, rsm, w, False)
            ys = _fox_decode_call(page_table, ck_t, cv_t, clf_t, j, qs_t[0], ks_t[0], vs_t[0], lfs_t[0])
            outs["k_p"].append(jnp.transpose(k_t, (0, 3, 1, 2)))
            outs["v_p"].append(jnp.transpose(v_t, (0, 3, 1, 2)))
            outs["lf_p"].append(jnp.transpose(lf_t, (0, 2, 1)))
            outs["k_s"].append(jnp.transpose(ks_t, (3, 0, 1, 2)))
            outs["v_s"].append(jnp.transpose(vs_t, (3, 0, 1, 2)))
            outs["lf_s"].append(jnp.transpose(lfs_t, (2, 0, 1)))
            xp = _out_ffn_call(xp, yp, wo, n2, rp, ffn)
            xs = _out_ffn_call(xs, ys[None], wo, n2, rsm, ffn)
        elif kind == 1:
            pw, ps = pool_w[j].astype(BF16), pool_scale[j][None]
            xp, tail = _pool_ffn_call(xp, n1, pw, ps, n2, _Rows(mods[i, :batch], batch, length, POOL_ROW_BLOCK), ffn)
            st_t = jnp.transpose(state_pool[j], (1, 0, 2))
            xs2, st_new = _pool_sample_call(xs[0], st_t, n1, pw, ps, n2, rsm.mods, ffn)
            xs = xs2[None]
            outs["pool_p"].append(tail[:, POOL_HALO - POOL_STATE:])
            outs["pool_s"].append(jnp.transpose(st_new, (1, 0, 2)))
        else:
            w = dict(wq=gla_wq[j].astype(BF16), wk=gla_wk[j].astype(BF16), wv=gla_wv[j].astype(BF16),
                     wa1=gla_wa1[j].astype(BF16), wa2=gla_wa2[j].astype(BF16), ba=gla_ba[j][None],
                     wr=gla_wr[j].astype(BF16))
            wo, gn = gla_wo[j].astype(BF16), gla_gn[j][None]
            q, k, v, g, r = _gla_proj_call(xp, n1, rp, w)
            yp, s_fin = _gla_scan_call(q, k, v, g, r, gn, 4 * GLA_CHUNK)
            q_t, k_t, vs_, g_t, r_s = _gla_sample_proj_call(xs[0], n1, rsm.mods, w)
            ys, s_new = _gla_decode_call(q_t, k_t, g_t, vs_[:, None], r_s[:, None], gn, state_gla[j])
            outs["gla_p"].append(s_fin)
            outs["gla_s"].append(s_new)
            xp = _out_ffn_call(xp, yp, wo, n2, rp, ffn)
            xs = _out_ffn_call(xs, ys.reshape(1, bd, d).astype(BF16), wo, n2, rsm, ffn)

    stack = lambda name: jnp.stack(outs[name])
    return (xp, xs.reshape(bd, 1, d),
            stack("k_p"), stack("v_p"), stack("lf_p"),
            stack("k_s"), stack("v_s"), stack("lf_s"),
            stack("pool_p"), stack("pool_s"), stack("gla_p"), stack("gla_s"))
```

```python
import functools
import math

import jax
import jax.numpy as jnp
import numpy as np
from jax import lax
from jax.experimental import pallas as pl
from jax.experimental.pallas import tpu as pltpu

F32, BF16 = jnp.float32, jnp.bfloat16

D_MODEL = 1024
N_MIXERS = 3
PAGE_SIZE = 128
FOX_HEADS = 16
FOX_HEAD_DIM = 64
FOX_SCALE = FOX_HEAD_DIM ** -0.5
LOG2E = math.log2(math.e)
FOX_AUG_ROWS = 16
POOL_WINDOWS = (2, 4, 8, 16)
POOL_GROUP_DIM = D_MODEL // len(POOL_WINDOWS)
POOL_STATE = max(POOL_WINDOWS) - 1
POOL_HALO = 16
GLA_HEADS = 4
GLA_DK = 128
GLA_DV = 256
GLA_TAU = 16.0
GLA_CHUNK = 64
D_FF = 2816
EPS = 1e-6

V7X_LANES = 128
V7X_VMEM_LIMIT_BYTES = 56 * 1024 * 1024

ROW_BLOCK = 512
POOL_ROW_BLOCK = 256
ATTN_BLOCK = 512
FF_CHUNK = 1408
ADA_COLS = 1536

NT_DIMS = (((1,), (1,)), ((), ()))
TN_DIMS = (((0,), (0,)), ((), ()))


def _params(*sem):
    return pltpu.CompilerParams(dimension_semantics=sem, vmem_limit_bytes=V7X_VMEM_LIMIT_BYTES)


def _resident(shape):
    zeros = (0,) * len(shape)
    return pl.BlockSpec(shape, lambda *_: zeros, pipeline_mode=pl.Buffered(1))


def _silu(x):
    return x * jax.nn.sigmoid(x)


def _log_sigmoid(x):
    return jnp.minimum(x, 0.0) - jnp.log1p(jnp.exp(-jnp.abs(x)))


def _rms(x, g):
    return x * lax.rsqrt(jnp.mean(x * x, axis=-1, keepdims=True) + EPS) * g


def _prenorm(x, g, shift, scale):
    return _rms(x, g) * (1.0 + scale) + shift


def _split3(x):
    hi = x.astype(BF16)
    r1 = x - hi.astype(F32)
    mid = r1.astype(BF16)
    lo = (r1 - mid.astype(F32)).astype(BF16)
    return hi, mid, lo


def _dot3(x, m01):
    hi, mid, lo = _split3(x)
    return (jnp.dot(hi, m01, preferred_element_type=F32)
            + jnp.dot(mid, m01, preferred_element_type=F32)
            + jnp.dot(lo, m01, preferred_element_type=F32))


def _dot3_left(m01, x):
    hi, mid, lo = _split3(x)
    return (jnp.dot(m01, hi, preferred_element_type=F32)
            + jnp.dot(m01, mid, preferred_element_type=F32)
            + jnp.dot(m01, lo, preferred_element_type=F32))


def _ffn_tail(x1, n2, shift2, scale2, gate2, wg_ref, wu_ref, wd_ref):
    f = _prenorm(x1, n2, shift2, scale2).astype(BF16)
    acc = jnp.zeros(x1.shape, F32)
    for c0 in range(0, D_FF, FF_CHUNK):
        g = jnp.dot(f, wg_ref[:, c0:c0 + FF_CHUNK], preferred_element_type=F32)
        u = jnp.dot(f, wu_ref[:, c0:c0 + FF_CHUNK], preferred_element_type=F32)
        a = (_silu(g) * u).astype(BF16)
        acc = acc + jnp.dot(a, wd_ref[c0:c0 + FF_CHUNK, :], preferred_element_type=F32)
    return x1 + gate2 * acc


def _ada_body(c_ref, w_ref, b_ref, o_ref):
    a = _silu(c_ref[...]).astype(BF16)
    o_ref[...] = jnp.dot(a, w_ref[...].astype(BF16), preferred_element_type=F32) + b_ref[...]


def _ada_call(c_all, ada_w, ada_b):
    depth, d, n = ada_w.shape
    rows = c_all.shape[0]
    return pl.pallas_call(
        _ada_body,
        grid=(depth, n // ADA_COLS),
        in_specs=[
            pl.BlockSpec((rows, d), lambda i, j: (0, 0)),
            pl.BlockSpec((None, d, ADA_COLS), lambda i, j: (i, 0, j)),
            pl.BlockSpec((None, 1, ADA_COLS), lambda i, j: (i, 0, j)),
        ],
        out_specs=pl.BlockSpec((None, rows, ADA_COLS), lambda i, j: (i, 0, j)),
        out_shape=jax.ShapeDtypeStruct((depth, rows, n), F32),
        compiler_params=_params("arbitrary", "arbitrary"),
        name="ada_mods",
    )(c_all, ada_w, ada_b.reshape(depth, 1, n))


class _Rows:
    def __init__(self, mods, batch, length, tm, per_row=False):
        self.batch, self.length, self.tm = batch, length, tm
        self.grid = (batch, length // tm)
        self.per_row = per_row
        self.mods = mods if per_row else mods.reshape(batch, 6, 1, D_MODEL)

    def rows(self, cols):
        return pl.BlockSpec((None, self.tm, cols), lambda b, i: (b, i, 0))

    def mod(self, c):
        if self.per_row:
            return pl.BlockSpec((self.tm, D_MODEL), lambda b, i: (i, c))
        return pl.BlockSpec((None, None, 1, D_MODEL), lambda b, i: (b, c, 0, 0))

    def shape(self, cols, dtype=F32):
        return jax.ShapeDtypeStruct((self.batch, self.length, cols), dtype)


def _head_norm(y, gain_col):
    ms = jnp.mean(y * y, axis=1, keepdims=True)
    return y * lax.rsqrt(ms + EPS) * gain_col[None, :, :]


def _fox_proj_body(x_ref, n1_ref, shift_ref, scale_ref, wq_ref, wk_ref, wv_ref, wf_ref, bf_ref,
                   qn_ref, kn_ref, tri_ref, *out_refs, with_bias_rows):
    tm = x_ref.shape[0]
    hb = _prenorm(x_ref[...], n1_ref[...], shift_ref[...], scale_ref[...]).astype(BF16)

    def proj_t(w_ref):
        return lax.dot_general(w_ref[...], hb, NT_DIMS, preferred_element_type=F32)

    q = _head_norm(proj_t(wq_ref).reshape(FOX_HEADS, FOX_HEAD_DIM, tm), qn_ref[...])
    k = _head_norm(proj_t(wk_ref).reshape(FOX_HEADS, FOX_HEAD_DIM, tm), kn_ref[...])
    v = proj_t(wv_ref).reshape(FOX_HEADS, FOX_HEAD_DIM, tm)
    lf = _log_sigmoid(proj_t(wf_ref) + bf_ref[...])

    if not with_bias_rows:
        q_ref, k_ref, v_ref, lf_ref = out_refs
        q_ref[...], k_ref[...], v_ref[...], lf_ref[...] = q, k, v, lf
        return

    qa_ref, ka_ref, k_ref, v_ref, lf_ref, carry_ref = out_refs
    k_ref[...], v_ref[...], lf_ref[...] = k, v, lf

    @pl.when(pl.program_id(1) == 0)
    def _():
        carry_ref[...] = jnp.zeros_like(carry_ref)

    f_cum = _dot3(lf, tri_ref[...]) + carry_ref[:, 0:1]
    carry_ref[...] = jnp.broadcast_to(f_cum[:, tm - 1:tm], carry_ref.shape)

    hi, mid, lo = (p.astype(F32)[:, None, :] for p in _split3(f_cum * LOG2E))
    r = lax.broadcasted_iota(jnp.int32, (FOX_HEADS, FOX_AUG_ROWS, tm), 1)
    q_rows = jnp.where(r == 0, hi, jnp.where(r == 1, mid, jnp.where(r == 2, lo, jnp.where(r < 6, 1.0, 0.0))))
    k_rows = jnp.where(r < 3, 1.0, jnp.where(r == 3, -hi, jnp.where(r == 4, -mid, jnp.where(r == 5, -lo, 0.0))))
    pad =jnp.zeros((FOX_HEADS, 2 * FOX_HEAD_DIM - FOX_HEAD_DIM - FOX_AUG_ROWS, tm), BF16)
    qa_ref[:, 0:FOX_HEAD_DIM, :] = (q * (FOX_SCALE * LOG2E)).astype(BF16)
    qa_ref[:, FOX_HEAD_DIM:FOX_HEAD_DIM + FOX_AUG_ROWS, :] = q_rows.astype(BF16)
    qa_ref[:, FOX_HEAD_DIM + FOX_AUG_ROWS:, :] = pad
    ka_ref[:, 0:FOX_HEAD_DIM, :] = k.astype(BF16)
    ka_ref[:, FOX_HEAD_DIM:FOX_HEAD_DIM + FOX_AUG_ROWS, :] = k_rows.astype(BF16)
    ka_ref[:, FOX_HEAD_DIM + FOX_AUG_ROWS:, :] = pad


def _fox_proj_call(x, n1, rs, w, with_bias_rows):
    batch, length, tm = rs.batch, rs.length, rs.tm
    hshape = lambda rows, dt: jax.ShapeDtypeStruct((batch, FOX_HEADS, rows, length), dt)
    hspec = lambda rows: pl.BlockSpec((None, FOX_HEADS, rows, tm), lambda b, i: (b, 0, 0, i))
    lf_shape = jax.ShapeDtypeStruct((batch, FOX_HEADS, length), F32)
    lf_spec = pl.BlockSpec((None, FOX_HEADS, tm), lambda b, i: (b, 0, i))
    if with_bias_rows:
        out_shape = (hshape(2 * FOX_HEAD_DIM, BF16), hshape(2 * FOX_HEAD_DIM, BF16),
                     hshape(FOX_HEAD_DIM, F32), hshape(FOX_HEAD_DIM, F32), lf_shape)
        out_specs = (hspec(2 * FOX_HEAD_DIM), hspec(2 * FOX_HEAD_DIM), hspec(FOX_HEAD_DIM), hspec(FOX_HEAD_DIM), lf_spec)
        scratch = [pltpu.VMEM((FOX_HEADS, V7X_LANES), F32)]
    else:
        out_shape = (hshape(FOX_HEAD_DIM, F32),) * 3 + (lf_shape,)
        out_specs = (hspec(FOX_HEAD_DIM),) * 3 + (lf_spec,)
        scratch = []
    tri = jnp.triu(jnp.ones((tm, tm), BF16))
    return pl.pallas_call(
        functools.partial(_fox_proj_body, with_bias_rows=with_bias_rows),
        grid=rs.grid,
        in_specs=[rs.rows(D_MODEL), _resident((1, D_MODEL)), rs.mod(0), rs.mod(1),
                  _resident((D_MODEL, D_MODEL)), _resident((D_MODEL, D_MODEL)), _resident((D_MODEL, D_MODEL)),
                  _resident((FOX_HEADS, D_MODEL)), _resident((FOX_HEADS, 1)),
                  _resident((FOX_HEAD_DIM, 1)), _resident((FOX_HEAD_DIM, 1)), _resident((tm, tm))],
        out_specs=out_specs,
        out_shape=out_shape,
        scratch_shapes=scratch,
        compiler_params=_params("arbitrary", "arbitrary"),
        name="fox_proj_bias" if with_bias_rows else "fox_proj",
    )(x, n1, rs.mods, rs.mods, w["wq_t"], w["wk_t"], w["wv_t"], w["wf_t"], w["bf"], w["qn"], w["kn"], tri)


def _fox_attn_body(qi_tab, ki_tab, q_ref, k_ref, v_ref, o_ref, qs_ref, m_ref, acc_ref):
    t = pl.program_id(2)
    qi, ki = qi_tab[t], ki_tab[t]
    tq, tk = qs_ref.shape[1], k_ref.shape[2]
    hd = FOX_HEAD_DIM
    den_lane = (hd, 0)

    @pl.when(ki == 0)
    def _():
        for hh in range(2):
            qs_ref[hh] = q_ref[hh].T
        m_ref[...] = jnp.full_like(m_ref, -jnp.inf)
        acc_ref[...] = jnp.zeros_like(acc_ref)

    def step(diagonal):
        vb = v_ref[...].astype(BF16)
        ones_row = (lax.broadcasted_iota(jnp.int32, (hd, tk), 0) == 0).astype(BF16)
        v_aug = (jnp.concatenate([vb[0:hd], ones_row], axis=0), jnp.concatenate([ones_row, vb[hd:]], axis=0))
        for hh in range(2):
            s = jnp.dot(qs_ref[hh], k_ref[hh], preferred_element_type=F32)
            if diagonal:
                row = lax.broadcasted_iota(jnp.int32, (tq, tk), 0)
                col = lax.broadcasted_iota(jnp.int32, (tq, tk), 1)
                s = jnp.where(col > row, -jnp.inf, s)
            m_prev = m_ref[hh]
            m_new = jnp.maximum(m_prev, jnp.max(s, axis=-1, keepdims=True))
            p = jnp.exp2(s - m_new[:, 0:1]).astype(BF16)
            m_ref[hh] = m_new
            acc_ref[hh] = (acc_ref[hh] * jnp.exp2(m_prev - m_new)
                           + lax.dot_general(p, v_aug[hh], NT_DIMS, preferred_element_type=F32))

    @pl.when(ki < qi)
    def _():
        step(False)

    @pl.when(ki == qi)
    def _():
        step(True)
        a0, a1 = acc_ref[0], acc_ref[1]
        first = lax.broadcasted_iota(jnp.int32, (tq, 2 * hd), 1) < hd
        o0 = a0 / a0[:, den_lane[0]:den_lane[0] + 1]
        o1 = a1 / a1[:, den_lane[1]:den_lane[1] + 1]
        o_ref[...] = jnp.where(first, o0, o1).astype(o_ref.dtype)


def _fox_attn_call(qa, ka, v_t):
    batch, _, _, length = qa.shape
    tq = tk = ATTN_BLOCK
    nq = length // tq
    pairs = [(i, j) for i in range(nq) for j in range(i + 1)]
    qi_tab = jnp.asarray([p[0] for p in pairs], jnp.int32)
    ki_tab = jnp.asarray([p[1] for p in pairs], jnp.int32)
    pair_rows = 2 * FOX_HEAD_DIM
    qa = qa.reshape(batch, FOX_HEADS // 2, 2, pair_rows, length)
    ka = ka.reshape(batch, FOX_HEADS // 2, 2, pair_rows, length)
    v_t = v_t.reshape(batch, FOX_HEADS // 2, pair_rows, length)
    return pl.pallas_call(
        _fox_attn_body,
        grid_spec=pltpu.PrefetchScalarGridSpec(
            num_scalar_prefetch=2,
            grid=(batch, FOX_HEADS // 2, len(pairs)),
            in_specs=[
                pl.BlockSpec((None, None, 2, pair_rows, tq), lambda b, h, t, qt, kt: (b, h, 0, 0, qt[t])),
                pl.BlockSpec((None, None, 2, pair_rows, tk), lambda b, h, t, qt, kt: (b, h, 0, 0, kt[t])),
                pl.BlockSpec((None, None, pair_rows, tk), lambda b, h, t, qt, kt: (b, h, 0, kt[t])),
            ],
            out_specs=pl.BlockSpec((None, tq, pair_rows), lambda b, h, t, qt, kt: (b, qt[t], h)),
            scratch_shapes=[pltpu.VMEM((2, tq, pair_rows), BF16), pltpu.VMEM((2, tq, V7X_LANES), F32),
                            pltpu.VMEM((2, tq, pair_rows), F32)],
        ),
        out_shape=jax.ShapeDtypeStruct((batch, length, D_MODEL), BF16),
        compiler_params=_params("arbitrary", "arbitrary", "arbitrary"),
        name="fox_attn",
    )(qi_tab, ki_tab, qa, ka, v_t)


def _lane_column(x_t, sel):
    return _dot3(x_t, sel)


def _fox_decode_body(pt_ref, *refs):
    n_pages = (len(refs) - 6) // 3
    ck_refs, cv_refs, clf_refs = refs[:n_pages], refs[n_pages:2 * n_pages], refs[2 * n_pages:3 * n_pages]
    q_ref, k_ref, v_ref, lf_ref, o_ref, ot_ref = refs[3 * n_pages:]
    b = pl.program_id(0)
    hs = (FOX_HEADS, FOX_HEAD_DIM, PAGE_SIZE)
    rows = FOX_HEADS * FOX_HEAD_DIM
    lane = lax.broadcasted_iota(jnp.int32, (FOX_HEADS, PAGE_SIZE), 1)

    @pl.when(b == 0)
    def _():
        ot_ref[...] = jnp.zeros_like(ot_ref)

    sel = (lax.broadcasted_iota(jnp.int32, (PAGE_SIZE, PAGE_SIZE), 0) == b).astype(BF16)
    qcol = _lane_column(q_ref[...].reshape(rows, PAGE_SIZE), sel).reshape(hs) * FOX_SCALE
    kcol = _lane_column(k_ref[...].reshape(rows, PAGE_SIZE), sel).reshape(hs)
    vcol = _lane_column(v_ref[...].reshape(rows, PAGE_SIZE), sel).reshape(hs)
    s_self = jnp.sum(qcol * kcol, axis=1)

    lf_all = jnp.concatenate([r[...] for r in clf_refs], axis=0)
    after = (lax.broadcasted_iota(jnp.int32, (PAGE_SIZE, PAGE_SIZE), 0)
             > lax.broadcasted_iota(jnp.int32, (PAGE_SIZE, PAGE_SIZE), 1)).astype(BF16)
    suffix_all = _dot3(lf_all, after)
    later = _lane_column(lf_ref[...], sel)
    scores = [None] * n_pages
    for p in reversed(range(n_pages)):
        pg = slice(p * FOX_HEADS, (p + 1) * FOX_HEADS)
        scores[p] = jnp.sum(qcol * ck_refs[p][...], axis=1) + later + suffix_all[pg]
        later = later + jnp.broadcast_to(suffix_all[pg, 0:1] + lf_all[pg, 0:1], lane.shape)

    m = s_self
    for s in scores:
        m = jnp.maximum(m, s)
    m = jnp.max(m, axis=-1, keepdims=True)
    p_self = jnp.exp(s_self - m)
    den = jnp.where(lane == 0, p_self, 0.0)
    acc = jnp.where(lane[:, None, :] == 0, vcol * p_self[:, None, :], 0.0)
    for p in range(n_pages):
        pr = jnp.exp(scores[p] - m)
        den = den + pr
        acc = acc + cv_refs[p][...] * pr[:, None, :]
    num = jnp.sum(acc, axis=-1, keepdims=True)
    den = jnp.sum(den, axis=-1, keepdims=True)[:, None, :]
    o = jnp.broadcast_to(num / den, hs)
    ot_ref[...] = jnp.where(lane[:, None, :] == b, o, ot_ref[...])

    @pl.when(b == pl.num_programs(0) - 1)
    def _():
        o_ref[...] = ot_ref[...].reshape(rows, PAGE_SIZE).T.astype(o_ref.dtype)


def _fox_decode_call(page_table, ck_t, cv_t, clf_t, j, q_t, k_t, v_t, lf_t):
    n_seq, n_pages = page_table.shape
    assert n_seq == PAGE_SIZE, "the sample batch is laid out on the 128 lanes"
    hs = (FOX_HEADS, FOX_HEAD_DIM, PAGE_SIZE)
    kv_page = lambda p: pl.BlockSpec((None, None) + hs, lambda b, pt: (j, pt[b, p], 0, 0, 0))
    lf_page = lambda p: pl.BlockSpec((None, None, FOX_HEADS, PAGE_SIZE), lambda b, pt: (j, pt[b, p], 0, 0))
    whole = lambda shape: pl.BlockSpec(shape, lambda b, pt: (0,) * len(shape))
    pages = range(n_pages)
    return pl.pallas_call(
        _fox_decode_body,
        grid_spec=pltpu.PrefetchScalarGridSpec(
            num_scalar_prefetch=1,
            grid=(n_seq,),
            in_specs=([kv_page(p) for p in pages] + [kv_page(p) for p in pages] + [lf_page(p) for p in pages]
                      + [whole(hs), whole(hs), whole(hs), whole((FOX_HEADS, PAGE_SIZE))]),
            out_specs=whole((n_seq, D_MODEL)),
            scratch_shapes=[pltpu.VMEM(hs, F32)],
        ),
        out_shape=jax.ShapeDtypeStruct((n_seq, D_MODEL), BF16),
        compiler_params=_params("arbitrary"),
        name="fox_decode",
    )(page_table, *([ck_t] * n_pages), *([cv_t] * n_pages), *([clf_t] * n_pages), q_t, k_t, v_t, lf_t)


def _out_ffn_body(x_ref, y_ref, wo_ref, gate1_ref, n2_ref, shift2_ref, scale2_ref, gate2_ref,
                  wg_ref, wu_ref, wd_ref, o_ref):
    mix = jnp.dot(y_ref[...], wo_ref[...], preferred_element_type=F32)
    x1 = x_ref[...] + gate1_ref[...] * mix
    o_ref[...] = _ffn_tail(x1, n2_ref[...], shift2_ref[...], scale2_ref[...], gate2_ref[...], wg_ref, wu_ref, wd_ref)


def _ffn_specs():
    return [_resident((D_MODEL, D_FF)), _resident((D_MODEL, D_FF)), _resident((D_FF, D_MODEL))]


def _out_ffn_call(x, y, wo, n2, rs, ffn):
    return pl.pallas_call(
        _out_ffn_body,
        grid=rs.grid,
        in_specs=[rs.rows(D_MODEL), rs.rows(D_MODEL), _resident((D_MODEL, D_MODEL)), rs.mod(2),
                  _resident((1, D_MODEL)), rs.mod(3), rs.mod(4), rs.mod(5)] + _ffn_specs(),
        out_specs=rs.rows(D_MODEL),
        out_shape=rs.shape(D_MODEL),
        compiler_params=_params("arbitrary", "arbitrary"),
        name="out_ffn",
    )(x, y, wo, rs.mods, n2, rs.mods, rs.mods, rs.mods, *ffn)


def _pool_groups(window_sums, h, inv_cnt, pw_ref, ps_ref):
    outs = []
    for g in range(len(POOL_WINDOWS)):
        sl = slice(g * POOL_GROUP_DIM, (g + 1) * POOL_GROUP_DIM)
        d = (window_sums[g] * inv_cnt[g] - h[:, sl]).astype(BF16)
        outs.append(jnp.dot(d, pw_ref[g], preferred_element_type=F32))
    return jnp.concatenate(outs, axis=-1) * ps_ref[...]


def _pool_ffn_body(x_ref, n1_ref, shift1_ref, scale1_ref, gate1_ref, pw_ref, ps_ref,
                   n2_ref, shift2_ref, scale2_ref, gate2_ref, wg_ref, wu_ref, wd_ref,
                   o_ref, tail_ref, ext_ref):
    tm = x_ref.shape[0]
    i = pl.program_id(1)
    x = x_ref[...]
    h = _prenorm(x, n1_ref[...], shift1_ref[...], scale1_ref[...])

    @pl.when(i == 0)
    def _():
        ext_ref[0:POOL_HALO, :] = jnp.zeros((POOL_HALO, D_MODEL), F32)

    ext_ref[POOL_HALO:, :] = h
    pos = i * tm + lax.broadcasted_iota(jnp.int32, (tm, 1), 0)
    sums, inv_cnt = [], []
    run, back = h, 1
    for g, win in enumerate(POOL_WINDOWS):
        lo = g * POOL_GROUP_DIM
        while back < win:
            run = run + ext_ref[POOL_HALO - back:POOL_HALO - back + tm, lo:]
            back += 1
        sums.append(run[:, :POOL_GROUP_DIM])
        if g + 1 < len(POOL_WINDOWS):
            run = run[:, POOL_GROUP_DIM:]
        inv_cnt.append(1.0 / jnp.minimum(win, pos + 1).astype(F32))
    y = _pool_groups(sums, h, inv_cnt, pw_ref, ps_ref)
    x1 = x + gate1_ref[...] * y
    o_ref[...] = _ffn_tail(x1, n2_ref[...], shift2_ref[...], scale2_ref[...], gate2_ref[...], wg_ref, wu_ref, wd_ref)
    halo = ext_ref[tm:tm + POOL_HALO, :]
    ext_ref[0:POOL_HALO, :] = halo
    tail_ref[...] = halo


def _pool_ffn_call(x, n1, pool_w, pool_scale, n2, rs, ffn):
    g = len(POOL_WINDOWS)
    return pl.pallas_call(
        _pool_ffn_body,
        grid=rs.grid,
        in_specs=[rs.rows(D_MODEL), _resident((1, D_MODEL)), rs.mod(0), rs.mod(1), rs.mod(2),
                  _resident((g, POOL_GROUP_DIM, POOL_GROUP_DIM)), _resident((1, D_MODEL)),
                  _resident((1, D_MODEL)), rs.mod(3), rs.mod(4), rs.mod(5)] + _ffn_specs(),
        out_specs=(rs.rows(D_MODEL), pl.BlockSpec((None, POOL_HALO, D_MODEL), lambda b, i: (b, 0, 0))),
        out_shape=(rs.shape(D_MODEL), jax.ShapeDtypeStruct((rs.batch, POOL_HALO, D_MODEL), F32)),
        scratch_shapes=[pltpu.VMEM((POOL_HALO + rs.tm, D_MODEL), F32)],
        compiler_params=_params("arbitrary", "arbitrary"),
        name="pool_ffn",
    )(x, n1, rs.mods, rs.mods, rs.mods, pool_w, pool_scale, n2, rs.mods, rs.mods, rs.mods, *ffn)


def _pool_sample_body(x_ref, st_ref, n1_ref, shift1_ref, scale1_ref, gate1_ref, pw_ref, ps_ref,
                      n2_ref, shift2_ref, scale2_ref, gate2_ref, wg_ref, wu_ref, wd_ref, o_ref, st_out_ref):
    x = x_ref[...]
    h = _prenorm(x, n1_ref[...], shift1_ref[...], scale1_ref[...])
    sums, inv_cnt = [], []
    run = h
    back = 1
    for g, win in enumerate(POOL_WINDOWS):
        lo = g * POOL_GROUP_DIM
        while back < win:
            run = run + st_ref[POOL_STATE - back]
            back += 1
        sums.append(run[:, lo:lo + POOL_GROUP_DIM])
        inv_cnt.append(1.0 / win)
    y = _pool_groups(sums, h, inv_cnt, pw_ref, ps_ref)
    x1 = x + gate1_ref[...] * y
    o_ref[...] = _ffn_tail(x1, n2_ref[...], shift2_ref[...], scale2_ref[...], gate2_ref[...], wg_ref, wu_ref, wd_ref)
    st_out_ref[0:POOL_STATE - 1] = st_ref[1:POOL_STATE]
    st_out_ref[POOL_STATE - 1] = h


def _pool_sample_call(x, state_t, n1, pool_w, pool_scale, n2, mods, ffn):
    bd = x.shape[0]
    g = len(POOL_WINDOWS)
    mod = lambda c: pl.BlockSpec((bd, D_MODEL), lambda i: (0, c), pipeline_mode=pl.Buffered(1))
    return pl.pallas_call(
        _pool_sample_body,
        grid=(1,),
        in_specs=[_resident((bd, D_MODEL)), _resident((POOL_STATE, bd, D_MODEL)), _resident((1, D_MODEL)),
                  mod(0), mod(1), mod(2), _resident((g, POOL_GROUP_DIM, POOL_GROUP_DIM)), _resident((1, D_MODEL)),
                  _resident((1, D_MODEL)), mod(3), mod(4), mod(5)] + _ffn_specs(),
        out_specs=(pl.BlockSpec((bd, D_MODEL), lambda i: (0, 0)),
                   pl.BlockSpec((POOL_STATE, bd, D_MODEL), lambda i: (0, 0, 0))),
        out_shape=(jax.ShapeDtypeStruct((bd, D_MODEL), F32), jax.ShapeDtypeStruct((POOL_STATE, bd, D_MODEL), F32)),
        compiler_params=_params("arbitrary"),
        name="pool_sample",
    )(x, state_t, n1, mods, mods, mods, pool_w, pool_scale, n2, mods, mods, mods, *ffn)


def _gla_gate(hb, wa1_ref, wa2_ref, ba):
    a1 = jnp.dot(hb, wa1_ref[...], preferred_element_type=F32).astype(BF16)
    return _log_sigmoid(jnp.dot(a1, wa2_ref[...], preferred_element_type=F32) + ba) / GLA_TAU


def _gla_proj_body(x_ref, n1_ref, shift_ref, scale_ref, wq_ref, wk_ref, wv_ref, wa1_ref, wa2_ref, ba_ref, wr_ref,
                   q_ref, k_ref, v_ref, g_ref, r_ref):
    hb = _prenorm(x_ref[...], n1_ref[...], shift_ref[...], scale_ref[...]).astype(BF16)
    q_ref[...] = jnp.dot(hb, wq_ref[...], preferred_element_type=F32) * (GLA_DK ** -0.5)
    k_ref[...] = jnp.dot(hb, wk_ref[...], preferred_element_type=F32)
    v_ref[...] = jnp.dot(hb, wv_ref[...], preferred_element_type=F32).astype(v_ref.dtype)
    g_ref[...] = _gla_gate(hb, wa1_ref, wa2_ref, ba_ref[...])
    r_ref[...] = _silu(jnp.dot(hb, wr_ref[...], preferred_element_type=F32))


def _gla_proj_call(x, n1, rs, w):
    kw = GLA_HEADS * GLA_DK
    rank = w["wa1"].shape[1]
    return pl.pallas_call(
        _gla_proj_body,
        grid=rs.grid,
        in_specs=[rs.rows(D_MODEL), _resident((1, D_MODEL)), rs.mod(0), rs.mod(1),
                  _resident((D_MODEL, kw)), _resident((D_MODEL, kw)), _resident((D_MODEL, D_MODEL)),
                  _resident((D_MODEL, rank)), _resident((rank, kw)), _resident((1, kw)), _resident((D_MODEL, D_MODEL))],
        out_specs=(rs.rows(kw), rs.rows(kw), rs.rows(D_MODEL), rs.rows(kw), rs.rows(D_MODEL)),
        out_shape=(rs.shape(kw), rs.shape(kw), rs.shape(D_MODEL, BF16), rs.shape(kw), rs.shape(D_MODEL)),
        compiler_params=_params("arbitrary", "arbitrary"),
        name="gla_proj",
    )(x, n1, rs.mods, rs.mods, w["wq"], w["wk"], w["wv"], w["wa1"], w["wa2"], w["ba"], w["wr"])


def _gla_out_norm(o, gn, r):
    return _rms(o, gn) * r


def _gla_scan_body(q_ref, k_ref, v_ref, g_ref, r_ref, gn_ref, y_ref, s_out_ref, st_ref):
    batch, rows = q_ref.shape[0], q_ref.shape[1]
    c = GLA_CHUNK

    @pl.when(pl.program_id(0) == 0)
    def _():
        st_ref[...] = jnp.zeros_like(st_ref)

    ri = lax.broadcasted_iota(jnp.int32, (c, c), 0)
    ci = lax.broadcasted_iota(jnp.int32, (c, c), 1)
    causal = ci <= ri
    tril = causal.astype(BF16)
    gn = gn_ref[...]

    def chunk(n, carry):
        r0 = pl.multiple_of(n * c, c)
        for b in range(batch):
            q, k, g = (ref[b, pl.ds(r0, c), :] for ref in (q_ref, k_ref, g_ref))
            v, rr = v_ref[b, pl.ds(r0, c), :], r_ref[b, pl.ds(r0, c), :]
            bcum = _dot3_left(tril, g)
            b_last = bcum[c - 1:c, :]
            e = jnp.exp(bcum)
            qe = (q * e).astype(BF16)
            ke = (k * jnp.exp(-bcum)).astype(BF16)
            kd = (k * jnp.exp(b_last - bcum)).astype(BF16)
            decay = jnp.exp(b_last)
            for h in range(GLA_HEADS):
                ks = slice(h * GLA_DK, (h + 1) * GLA_DK)
                vs = slice(h * GLA_DV, (h + 1) * GLA_DV)
                st = st_ref[b, h]
                a = lax.dot_general(qe[:, ks], ke[:, ks], NT_DIMS, preferred_element_type=F32)
                a = jnp.where(causal, a, 0.0).astype(BF16)
                o = (jnp.dot(a, v[:, vs], preferred_element_type=F32)
                     + lax.dot_general(qe[:, ks], st.astype(BF16), NT_DIMS, preferred_element_type=F32))
                st_ref[b, h] = st * decay[:, ks] + lax.dot_general(v[:, vs], kd[:, ks], TN_DIMS,
                                                                  preferred_element_type=F32)
                y_ref[b, pl.ds(r0, c), vs] = _gla_out_norm(o, gn, rr[:, vs]).astype(BF16)
        return carry

    lax.fori_loop(0, rows // c, chunk, 0)

    @pl.when(pl.program_id(0) == pl.num_programs(0) - 1)
    def _():
        for b in range(batch):
            for h in range(GLA_HEADS):
                s_out_ref[b, h] = st_ref[b, h].T


def _gla_scan_call(q, k, v, g, r, gn, rows):
    batch, length, kw = q.shape
    blk = lambda cols: pl.BlockSpec((batch, rows, cols), lambda i: (0, i, 0))
    s_shape = (batch, GLA_HEADS, GLA_DK, GLA_DV)
    return pl.pallas_call(
        _gla_scan_body,
        grid=(length // rows,),
        in_specs=[blk(kw), blk(kw), blk(D_MODEL), blk(kw), blk(D_MODEL), _resident((1, GLA_DV))],
        out_specs=(blk(D_MODEL), pl.BlockSpec(s_shape, lambda i: (0, 0, 0, 0))),
        out_shape=(jax.ShapeDtypeStruct((batch, length, D_MODEL), BF16), jax.ShapeDtypeStruct(s_shape, F32)),
        scratch_shapes=[pltpu.VMEM((batch, GLA_HEADS, GLA_DV, GLA_DK), F32)],
        compiler_params=_params("arbitrary"),
        name="gla_scan",
    )(q, k, v, g, r, gn)


def _gla_sample_proj_body(x_ref, n1_ref, shift_ref, scale_ref, wq_ref, wk_ref, wv_ref, wa1_ref, wa2_ref, ba_ref,
                          wr_ref, q_ref, k_ref, v_ref, g_ref, r_ref):
    hb = _prenorm(x_ref[...], n1_ref[...], shift_ref[...], scale_ref[...]).astype(BF16)
    q_ref[...] = jnp.dot(hb, wq_ref[...], preferred_element_type=F32).T * (GLA_DK ** -0.5)
    k_ref[...] = jnp.dot(hb, wk_ref[...], preferred_element_type=F32).T
    v_ref[...] = jnp.dot(hb, wv_ref[...], preferred_element_type=F32)
    g_ref[...] = _gla_gate(hb, wa1_ref, wa2_ref, ba_ref[...]).T
    r_ref[...] = _silu(jnp.dot(hb, wr_ref[...], preferred_element_type=F32))


def _gla_sample_proj_call(x, n1, mods, w):
    bd = x.shape[0]
    kw = GLA_HEADS * GLA_DK
    rank = w["wa1"].shape[1]
    mod = lambda c: pl.BlockSpec((bd, D_MODEL), lambda i: (0, c), pipeline_mode=pl.Buffered(1))
    full = lambda *s: pl.BlockSpec(s, lambda i: (0,) * len(s))
    return pl.pallas_call(
        _gla_sample_proj_body,
        grid=(1,),
        in_specs=[_resident((bd, D_MODEL)), _resident((1, D_MODEL)), mod(0), mod(1),
                  _resident((D_MODEL, kw)), _resident((D_MODEL, kw)), _resident((D_MODEL, D_MODEL)),
                  _resident((D_MODEL, rank)), _resident((rank, kw)), _resident((1, kw)), _resident((D_MODEL, D_MODEL))],
        out_specs=(full(kw, bd), full(kw, bd), full(bd, D_MODEL), full(kw, bd), full(bd, D_MODEL)),
        out_shape=(jax.ShapeDtypeStruct((kw, bd), F32), jax.ShapeDtypeStruct((kw, bd), F32),
                   jax.ShapeDtypeStruct((bd, D_MODEL), F32), jax.ShapeDtypeStruct((kw, bd), F32),
                   jax.ShapeDtypeStruct((bd, D_MODEL), F32)),
        compiler_params=_params("arbitrary"),
        name="gla_sample_proj",
    )(x, n1, mods, mods, w["wq"], w["wk"], w["wv"], w["wa1"], w["wa2"], w["ba"], w["wr"])


def _gla_decode_body(q_ref, k_ref, g_ref, v_ref, r_ref, gn_ref, s_ref, y_ref, s_out_ref):
    b = pl.program_id(0)
    n = q_ref.shape[1]
    sel = (lax.broadcasted_iota(jnp.int32, (n, n), 0) == b).astype(BF16)
    wide = lambda col: jnp.concatenate([col, col], axis=-1).reshape(GLA_HEADS, GLA_DK, GLA_DV)
    qcol = wide(_lane_column(q_ref[...], sel))
    kcol = wide(_lane_column(k_ref[...], sel))
    decay = wide(jnp.exp(_lane_column(g_ref[...], sel)))
    gn = gn_ref[...]
    for h in range(GLA_HEADS):
        vs = slice(h * GLA_DV, (h + 1) * GLA_DV)
        s_new = s_ref[h] * decay[h] + kcol[h] * v_ref[:, vs]
        s_out_ref[h] = s_new
        o = jnp.sum(qcol[h] * s_new, axis=0, keepdims=True)
        y_ref[:, vs] = _gla_out_norm(o, gn, r_ref[:, vs])


def _gla_decode_call(q_t, k_t, g_t, v, r, gn, state):
    kw, bd = q_t.shape
    assert bd == V7X_LANES, "the sample batch is laid out on the 128 lanes"
    full = lambda *s: pl.BlockSpec(s, lambda i: (0,) * len(s))
    row = pl.BlockSpec((None, 1, D_MODEL), lambda i: (i, 0, 0))
    st = pl.BlockSpec((None, GLA_HEADS, GLA_DK, GLA_DV), lambda i: (i, 0, 0, 0))
    return pl.pallas_call(
        _gla_decode_body,
        grid=(bd,),
        in_specs=[full(kw, bd), full(kw, bd), full(kw, bd), row, row, full(1, GLA_DV), st],
        out_specs=(row, st),
        out_shape=(jax.ShapeDtypeStruct((bd, 1, D_MODEL), F32), jax.ShapeDtypeStruct(state.shape, F32)),
        compiler_params=_params("arbitrary"),
        name="gla_decode",
    )(q_t, k_t, g_t, v, r, gn, state)


def kernel(x_prompt, x_sample, cache_k, cache_v, cache_logf, state_pool, state_gla, page_table, c_prompt, c_sample, norm1, norm2, ada_w, ada_b, fox_wq, fox_wk, fox_wv, fox_wf, fox_bf, fox_qn, fox_kn, fox_wo, pool_w, pool_scale, gla_wq, gla_wk, gla_wv, gla_wa1, gla_wa2, gla_ba, gla_wr, gla_gn, gla_wo, ffn_wg, ffn_wu, ffn_wd):
    batch, length, d = x_prompt.shape
    bd = x_sample.shape[0]
    depth = norm1.shape[0]
    assert d == D_MODEL and x_sample.shape[1] == 1

    c_all = jnp.concatenate([c_prompt, c_sample], axis=0)
    c_all = jnp.pad(c_all, ((0, -c_all.shape[0] % 8), (0, 0)))
    mods = _ada_call(c_all, ada_w, ada_b)

    ck_t = jnp.transpose(cache_k, (0, 1, 3, 4, 2))
    cv_t = jnp.transpose(cache_v, (0, 1, 3, 4, 2))
    clf_t = jnp.transpose(cache_logf, (0, 1, 3, 2))

    xp = x_prompt
    xs = x_sample.reshape(1, bd, d)
    outs = {k: [] for k in ("k_p", "v_p", "lf_p", "k_s", "v_s", "lf_s", "pool_p", "pool_s", "gla_p", "gla_s")}
    for i in range(depth):
        kind, j = i % N_MIXERS, i // N_MIXERS
        rp = _Rows(mods[i, :batch], batch, length, ROW_BLOCK)
        rsm = _Rows(mods[i, batch:batch + bd], 1, bd, bd, per_row=True)
        n1, n2 = norm1[i][None], norm2[i][None]
        ffn = (ffn_wg[i].astype(BF16), ffn_wu[i].astype(BF16), ffn_wd[i].astype(BF16))
        if kind == 0:
            w = dict(wq_t=fox_wq[j].T.astype(BF16), wk_t=fox_wk[j].T.astype(BF16), wv_t=fox_wv[j].T.astype(BF16),
                     wf_t=fox_wf[j].T.astype(BF16), bf=fox_bf[j][:, None], qn=fox_qn[j][:, None], kn=fox_kn[j][:, None])
            wo = fox_wo[j].astype(BF16)
            qa, ka, k_t, v_t, lf_t = _fox_proj_call(xp, n1, rp, w, True)
            yp = _fox_attn_call(qa, ka, v_t)
            qs_t, ks_t, vs_t, lfs_t = _fox_proj_call(xs, n1, rsm, w, False)
            ys = _fox_decode_call(page_table, ck_t, cv_t, clf_t, j, qs_t[0], ks_t[0], vs_t[0], lfs_t[0])
            outs["k_p"].append(jnp.transpose(k_t, (0, 3, 1, 2)))
            outs["v_p"].append(jnp.transpose(v_t, (0, 3, 1, 2)))
            outs["lf_p"].append(jnp.transpose(lf_t, (0, 2, 1)))
            outs["k_s"].append(jnp.transpose(ks_t, (3, 0, 1, 2)))
            outs["v_s"].append(jnp.transpose(vs_t, (3, 0, 1, 2)))
            outs["lf_s"].append(jnp.transpose(lfs_t, (2, 0, 1)))
            xp = _out_ffn_call(xp, yp, wo, n2, rp, ffn)
            xs = _out_ffn_call(xs, ys[None], wo, n2, rsm, ffn)
        elif kind == 1:
            pw, ps = pool_w[j].astype(BF16), pool_scale[j][None]
            xp, tail = _pool_ffn_call(xp, n1, pw, ps, n2, _Rows(mods[i, :batch], batch, length, POOL_ROW_BLOCK), ffn)
            st_t = jnp.transpose(state_pool[j], (1, 0, 2))
            xs2, st_new = _pool_sample_call(xs[0], st_t, n1, pw, ps, n2, rsm.mods, ffn)
            xs = xs2[None]
            outs["pool_p"].append(tail[:, POOL_HALO - POOL_STATE:])
            outs["pool_s"].append(jnp.transpose(st_new, (1, 0, 2)))
        else:
            w = dict(wq=gla_wq[j].astype(BF16), wk=gla_wk[j].astype(BF16), wv=gla_wv[j].astype(BF16),
                     wa1=gla_wa1[j].astype(BF16), wa2=gla_wa2[j].astype(BF16), ba=gla_ba[j][None],
                     wr=gla_wr[j].astype(BF16))
            wo, gn = gla_wo[j].astype(BF16), gla_gn[j][None]
            q, k, v, g, r = _gla_proj_call(xp, n1, rp, w)
            yp, s_fin = _gla_scan_call(q, k, v, g, r, gn, 4 * GLA_CHUNK)
            q_t, k_t, vs_, g_t, r_s = _gla_sample_proj_call(xs[0], n1, rsm.mods, w)
            ys, s_new = _gla_decode_call(q_t, k_t, g_t, vs_[:, None], r_s[:, None], gn, state_gla[j])
            outs["gla_p"].append(s_fin)
            outs["gla_s"].append(s_new)
            xp = _out_ffn_call(xp, yp, wo, n2, rp, ffn)
            xs = _out_ffn_call(xs, ys.reshape(1, bd, d).astype(BF16), wo, n2, rsm, ffn)

    stack = lambda name: jnp.stack(outs[name])
    return (xp, xs.reshape(bd, 1, d),
            stack("k_p"), stack("v_p"), stack("lf_p"),
            stack("k_s"), stack("v_s"), stack("lf_s"),
            stack("pool_p"), stack("pool_s"), stack("gla_p"), stack("gla_s"))
```

```python
import functools
import math

import jax
import jax.numpy as jnp
import numpy as np
from jax import lax
from jax.experimental import pallas as pl
from jax.experimental.pallas import tpu as pltpu

F32, BF16 = jnp.float32, jnp.bfloat16

D_MODEL = 1024
N_MIXERS = 3
PAGE_SIZE = 128
FOX_HEADS = 16
FOX_HEAD_DIM = 64
FOX_SCALE = FOX_HEAD_DIM ** -0.5
LOG2E = math.log2(math.e)
FOX_AUG_ROWS = 16
POOL_WINDOWS = (2, 4, 8, 16)
POOL_GROUP_DIM = D_MODEL // len(POOL_WINDOWS)
POOL_STATE = max(POOL_WINDOWS) - 1
POOL_HALO = 16
GLA_HEADS = 4
GLA_DK = 128
GLA_DV = 256
GLA_TAU = 16.0
GLA_CHUNK = 64
D_FF = 2816
EPS = 1e-6

V7X_LANES = 128
V7X_VMEM_LIMIT_BYTES = 56 * 1024 * 1024

ROW_BLOCK = 512
POOL_ROW_BLOCK = 256
ATTN_BLOCK = 512
ATTN_HEADS = 4
V7X_MXU_DIM = 256
FF_CHUNK = 6 * V7X_MXU_DIM
ADA_COLS = 1536

NT_DIMS = (((1,), (1,)), ((), ()))
TN_DIMS = (((0,), (0,)), ((), ()))


def _params(*sem):
    return pltpu.CompilerParams(dimension_semantics=sem, vmem_limit_bytes=V7X_VMEM_LIMIT_BYTES)


def _resident(shape):
    zeros = (0,) * len(shape)
    return pl.BlockSpec(shape, lambda *_: zeros, pipeline_mode=pl.Buffered(1))


def _silu(x):
    return x * jax.nn.sigmoid(x)


def _log_sigmoid(x):
    return jnp.minimum(x, 0.0) - jnp.log1p(jnp.exp(-jnp.abs(x)))


def _rms(x, g):
    return x * lax.rsqrt(jnp.mean(x * x, axis=-1, keepdims=True) + EPS) * g


def _prenorm(x, g, shift, scale):
    return _rms(x, g) * (1.0 + scale) + shift


def _split3(x):
    hi = x.astype(BF16)
    r1 = x - hi.astype(F32)
    mid = r1.astype(BF16)
    lo = (r1 - mid.astype(F32)).astype(BF16)
    return hi, mid, lo


def _dot3(x, m01):
    hi, mid, lo = _split3(x)
    return (jnp.dot(hi, m01, preferred_element_type=F32)
            + jnp.dot(mid, m01, preferred_element_type=F32)
            + jnp.dot(lo, m01, preferred_element_type=F32))


def _dot3_left(m01, x):
    hi, mid, lo = _split3(x)
    return (jnp.dot(m01, hi, preferred_element_type=F32)
            + jnp.dot(m01, mid, preferred_element_type=F32)
            + jnp.dot(m01, lo, preferred_element_type=F32))


def _ffn_tail(x1, n2, shift2, scale2, gate2, wg_ref, wu_ref, wd_ref):
    f = _prenorm(x1, n2, shift2, scale2).astype(BF16)
    acc = jnp.zeros(x1.shape, F32)
    for c0 in range(0, D_FF, FF_CHUNK):
        c1 = min(c0 + FF_CHUNK, D_FF)
        g = jnp.dot(f, wg_ref[:, c0:c1], preferred_element_type=F32)
        u = jnp.dot(f, wu_ref[:, c0:c1], preferred_element_type=F32)
        a = (_silu(g) * u).astype(BF16)
        acc = acc + jnp.dot(a, wd_ref[c0:c1, :], preferred_element_type=F32)
    return x1 + gate2 * acc


def _ada_body(c_ref, w_ref, b_ref, o_ref):
    a = _silu(c_ref[...]).astype(BF16)
    o_ref[...] = jnp.dot(a, w_ref[...].astype(BF16), preferred_element_type=F32) + b_ref[...]


def _ada_call(c_all, ada_w, ada_b):
    depth, d, n = ada_w.shape
    rows = c_all.shape[0]
    return pl.pallas_call(
        _ada_body,
        grid=(depth, n // ADA_COLS),
        in_specs=[
            pl.BlockSpec((rows, d), lambda i, j: (0, 0)),
            pl.BlockSpec((None, d, ADA_COLS), lambda i, j: (i, 0, j)),
            pl.BlockSpec((None, 1, ADA_COLS), lambda i, j: (i, 0, j)),
        ],
        out_specs=pl.BlockSpec((None, rows, ADA_COLS), lambda i, j: (i, 0, j)),
        out_shape=jax.ShapeDtypeStruct((depth, rows, n), F32),
        compiler_params=_params("arbitrary", "arbitrary"),
        name="ada_mods",
    )(c_all, ada_w, ada_b.reshape(depth, 1, n))


class _Rows:
    def __init__(self, mods, batch, length, tm, per_row=False):
        self.batch, self.length, self.tm = batch, length, tm
        self.grid = (batch, length // tm)
        self.per_row = per_row
        self.mods = mods if per_row else mods.reshape(batch, 6, 1, D_MODEL)

    def rows(self, cols):
        return pl.BlockSpec((None, self.tm, cols), lambda b, i: (b, i, 0))

    def mod(self, c):
        if self.per_row:
            return pl.BlockSpec((self.tm, D_MODEL), lambda b, i: (i, c))
        return pl.BlockSpec((None, None, 1, D_MODEL), lambda b, i: (b, c, 0, 0))

    def shape(self, cols, dtype=F32):
        return jax.ShapeDtypeStruct((self.batch, self.length, cols), dtype)


def _head_norm(y, gain_col):
    ms = jnp.mean(y * y, axis=1, keepdims=True)
    return y * lax.rsqrt(ms + EPS) * gain_col[None, :, :]


def _fox_proj_body(x_ref, n1_ref, shift_ref, scale_ref, wq_ref, wk_ref, wv_ref, wf_ref, bf_ref,
                   qn_ref, kn_ref, tri_ref, *out_refs, with_bias_rows):
    tm = x_ref.shape[0]
    hb = _prenorm(x_ref[...], n1_ref[...], shift_ref[...], scale_ref[...]).astype(BF16)

    def proj_t(w_ref):
        return lax.dot_general(w_ref[...], hb, NT_DIMS, preferred_element_type=F32)

    q = _head_norm(proj_t(wq_ref).reshape(FOX_HEADS, FOX_HEAD_DIM, tm), qn_ref[...])
    k = _head_norm(proj_t(wk_ref).reshape(FOX_HEADS, FOX_HEAD_DIM, tm), kn_ref[...])
    v = proj_t(wv_ref).reshape(FOX_HEADS, FOX_HEAD_DIM, tm)
    lf = _log_sigmoid(proj_t(wf_ref) + bf_ref[...])

    if not with_bias_rows:
        q_ref, k_ref, v_ref, lf_ref = out_refs
        q_ref[...], k_ref[...], v_ref[...], lf_ref[...] = q, k, v, lf
        return

    qa_ref, kr_ref, k_ref, v_ref, lf_ref, carry_ref = out_refs
    k_ref[...], v_ref[...], lf_ref[...] = k, v, lf

    @pl.when(pl.program_id(1) == 0)
    def _():
        carry_ref[...] = jnp.zeros_like(carry_ref)

    f_cum = _dot3(lf, tri_ref[...]) + carry_ref[:, 0:1]
    carry_ref[...] = jnp.broadcast_to(f_cum[:, tm - 1:tm], carry_ref.shape)

    hi, mid, lo = (p.astype(F32)[:, None, :] for p in _split3(f_cum * LOG2E))
    r = lax.broadcasted_iota(jnp.int32, (FOX_HEADS, FOX_AUG_ROWS, tm), 1)
    q_rows = jnp.where(r == 0, hi, jnp.where(r == 1, mid, jnp.where(r == 2, lo, jnp.where(r < 6, 1.0, 0.0))))
    k_rows = jnp.where(r < 3, 1.0, jnp.where(r == 3, -hi, jnp.where(r == 4, -mid, jnp.where(r == 5, -lo, 0.0))))
    pad =jnp.zeros((FOX_HEADS, 2 * FOX_HEAD_DIM - FOX_HEAD_DIM - FOX_AUG_ROWS, tm), BF16)
    qa_ref[:, 0:FOX_HEAD_DIM, :] = (q * (FOX_SCALE * LOG2E)).astype(BF16)
    qa_ref[:, FOX_HEAD_DIM:FOX_HEAD_DIM + FOX_AUG_ROWS, :] = q_rows.astype(BF16)
    qa_ref[:, FOX_HEAD_DIM + FOX_AUG_ROWS:, :] = pad
    ka = jnp.concatenate([k.astype(BF16), k_rows.astype(BF16), pad], axis=1)
    for h in range(FOX_HEADS):
        kr_ref[h] = ka[h].T


def _fox_proj_call(x, n1, rs, w, with_bias_rows):
    batch, length, tm = rs.batch, rs.length, rs.tm
    hshape = lambda rows, dt: jax.ShapeDtypeStruct((batch, FOX_HEADS, rows, length), dt)
    hspec = lambda rows: pl.BlockSpec((None, FOX_HEADS, rows, tm), lambda b, i: (b, 0, 0, i))
    lf_shape = jax.ShapeDtypeStruct((batch, FOX_HEADS, length), F32)
    lf_spec = pl.BlockSpec((None, FOX_HEADS, tm), lambda b, i: (b, 0, i))
    if with_bias_rows:
        kr_shape = jax.ShapeDtypeStruct((batch, FOX_HEADS, length, 2 * FOX_HEAD_DIM), BF16)
        kr_spec = pl.BlockSpec((None, FOX_HEADS, tm, 2 * FOX_HEAD_DIM), lambda b, i: (b, 0, i, 0))
        out_shape = (hshape(2 * FOX_HEAD_DIM, BF16), kr_shape, hshape(FOX_HEAD_DIM, F32), hshape(FOX_HEAD_DIM, F32), lf_shape)
        out_specs = (hspec(2 * FOX_HEAD_DIM), kr_spec, hspec(FOX_HEAD_DIM), hspec(FOX_HEAD_DIM), lf_spec)
        scratch = [pltpu.VMEM((FOX_HEADS, V7X_LANES), F32)]
    else:
        out_shape = (hshape(FOX_HEAD_DIM, F32),) * 3 + (lf_shape,)
        out_specs = (hspec(FOX_HEAD_DIM),) * 3 + (lf_spec,)
        scratch = []
    tri = jnp.triu(jnp.ones((tm, tm), BF16))
    return pl.pallas_call(
        functools.partial(_fox_proj_body, with_bias_rows=with_bias_rows),
        grid=rs.grid,
        in_specs=[rs.rows(D_MODEL), _resident((1, D_MODEL)), rs.mod(0), rs.mod(1),
                  _resident((D_MODEL, D_MODEL)), _resident((D_MODEL, D_MODEL)), _resident((D_MODEL, D_MODEL)),
                  _resident((FOX_HEADS, D_MODEL)), _resident((FOX_HEADS, 1)),
                  _resident((FOX_HEAD_DIM, 1)), _resident((FOX_HEAD_DIM, 1)), _resident((tm, tm))],
        out_specs=out_specs,
        out_shape=out_shape,
        scratch_shapes=scratch,
        compiler_params=_params("arbitrary", "arbitrary"),
        name="fox_proj_bias" if with_bias_rows else "fox_proj",
    )(x, n1, rs.mods, rs.mods, w["wq_t"], w["wk_t"], w["wv_t"], w["wf_t"], w["bf"], w["qn"], w["kn"], tri)


def _fox_attn_body(qi_tab, ki_tab, q_ref, k_ref, v_ref, o_ref, m_ref, acc_ref):
    t = pl.program_id(2)
    qi, ki = qi_tab[t], ki_tab[t]
    nh, tk, tq = k_ref.shape[0], k_ref.shape[1], q_ref.shape[2]
    hd = FOX_HEAD_DIM

    @pl.when(ki == 0)
    def _():
        m_ref[...] = jnp.full_like(m_ref, -jnp.inf)
        acc_ref[...] = jnp.zeros_like(acc_ref)

    def step(diagonal):
        scores = [jnp.dot(k_ref[hh], q_ref[hh], preferred_element_type=F32) for hh in range(nh)]
        ones_row = (lax.broadcasted_iota(jnp.int32, (hd, tk), 0) == 0).astype(BF16)
        for hh in range(nh):
            s = scores[hh]
            if diagonal:
                key = lax.broadcasted_iota(jnp.int32, (tk, tq), 0)
                qry = lax.broadcasted_iota(jnp.int32, (tk, tq), 1)
                s = jnp.where(key > qry, -jnp.inf, s)
            m_prev = m_ref[hh]
            m_new = jnp.maximum(m_prev, jnp.max(s, axis=0, keepdims=True))
            p = jnp.exp2(s - m_new).astype(BF16)
            m_ref[hh] = m_new
            vb = v_ref[hh * hd:(hh + 1) * hd, :].astype(BF16)
            v_aug = jnp.concatenate([vb, ones_row] if hh % 2 == 0 else [ones_row, vb], axis=0)
            acc_ref[hh] = (acc_ref[hh] * jnp.exp2(m_prev - m_new)
                           + jnp.dot(v_aug, p, preferred_element_type=F32))

    @pl.when(ki < qi)
    def _():
        step(False)

    @pl.when(ki == qi)
    def _():
        step(True)
        first = lax.broadcasted_iota(jnp.int32, (2 * hd, tq), 0) < hd
        for pair in range(nh // 2):
            a0, a1 = acc_ref[2 * pair], acc_ref[2 * pair + 1]
            o_t = jnp.where(first, a0 / a0[hd:hd + 1, :], a1 / a1[0:1, :])
            o_ref[:, pair * 2 * hd:(pair + 1) * 2 * hd] = o_t.T.astype(o_ref.dtype)


def _fox_attn_call(qa, kr, v_t):
    batch, _, rows, length = qa.shape
    tq = tk = ATTN_BLOCK
    nh = ATTN_HEADS
    groups = FOX_HEADS // nh
    nq = length // tq
    pairs = [(i, j) for i in range(nq) for j in range(i + 1)]
    qi_tab = jnp.asarray([p[0] for p in pairs], jnp.int32)
    ki_tab = jnp.asarray([p[1] for p in pairs], jnp.int32)
    qa = qa.reshape(batch, groups, nh, rows, length)
    kr = kr.reshape(batch, groups, nh, length, rows)
    v_t = v_t.reshape(batch, groups, nh * FOX_HEAD_DIM, length)
    return pl.pallas_call(
        _fox_attn_body,
        grid_spec=pltpu.PrefetchScalarGridSpec(
            num_scalar_prefetch=2,
            grid=(batch, groups, len(pairs)),
            in_specs=[
                pl.BlockSpec((None, None, nh, rows, tq), lambda b, h, t, qt, kt: (b, h, 0, 0, qt[t])),
                pl.BlockSpec((None, None, nh, tk, rows), lambda b, h, t, qt, kt: (b, h, 0, kt[t], 0)),
                pl.BlockSpec((None, None, nh * FOX_HEAD_DIM, tk), lambda b, h, t, qt, kt: (b, h, 0, kt[t])),
            ],
            out_specs=pl.BlockSpec((None, tq, nh * FOX_HEAD_DIM), lambda b, h, t, qt, kt: (b, qt[t], h)),
            scratch_shapes=[pltpu.VMEM((nh, 1, tq), F32), pltpu.VMEM((nh, rows, tq), F32)],
        ),
        out_shape=jax.ShapeDtypeStruct((batch, length, D_MODEL), BF16),
        compiler_params=_params("arbitrary", "arbitrary", "arbitrary"),
        name="fox_attn",
    )(qi_tab, ki_tab, qa, kr, v_t)


def _lane_column(x_t, sel):
    return _dot3(x_t, sel)


def _fox_decode_body(pt_ref, *refs):
    n_pages = (len(refs) - 6) // 3
    ck_refs, cv_refs, clf_refs = refs[:n_pages], refs[n_pages:2 * n_pages], refs[2 * n_pages:3 * n_pages]
    q_ref, k_ref, v_ref, lf_ref, o_ref, ot_ref = refs[3 * n_pages:]
    b = pl.program_id(0)
    hs = (FOX_HEADS, FOX_HEAD_DIM, PAGE_SIZE)
    rows = FOX_HEADS * FOX_HEAD_DIM
    lane = lax.broadcasted_iota(jnp.int32, (FOX_HEADS, PAGE_SIZE), 1)

    @pl.when(b == 0)
    def _():
        ot_ref[...] = jnp.zeros_like(ot_ref)

    sel = (lax.broadcasted_iota(jnp.int32, (PAGE_SIZE, PAGE_SIZE), 0) == b).astype(BF16)
    qcol = _lane_column(q_ref[...].reshape(rows, PAGE_SIZE), sel).reshape(hs) * FOX_SCALE
    kcol = _lane_column(k_ref[...].reshape(rows, PAGE_SIZE), sel).reshape(hs)
    vcol = _lane_column(v_ref[...].reshape(rows, PAGE_SIZE), sel).reshape(hs)
    s_self = jnp.sum(qcol * kcol, axis=1)

    lf_all = jnp.concatenate([r[...] for r in clf_refs], axis=0)
    after = (lax.broadcasted_iota(jnp.int32, (PAGE_SIZE, PAGE_SIZE), 0)
             > lax.broadcasted_iota(jnp.int32, (PAGE_SIZE, PAGE_SIZE), 1)).astype(BF16)
    suffix_all = _dot3(lf_all, after)
    later = _lane_column(lf_ref[...], sel)
    scores = [None] * n_pages
    for p in reversed(range(n_pages)):
        pg = slice(p * FOX_HEADS, (p + 1) * FOX_HEADS)
        scores[p] = jnp.sum(qcol * ck_refs[p][...], axis=1) + later + suffix_all[pg]
        later = later + jnp.broadcast_to(suffix_all[pg, 0:1] + lf_all[pg, 0:1], lane.shape)

    m = s_self
    for s in scores:
        m = jnp.maximum(m, s)
    m = jnp.max(m, axis=-1, keepdims=True)
    p_self = jnp.exp(s_self - m)
    den = jnp.where(lane == 0, p_self, 0.0)
    acc = jnp.where(lane[:, None, :] == 0, vcol * p_self[:, None, :], 0.0)
    for p in range(n_pages):
        pr = jnp.exp(scores[p] - m)
        den = den + pr
        acc = acc + cv_refs[p][...] * pr[:, None, :]
    num = jnp.sum(acc, axis=-1, keepdims=True)
    den = jnp.sum(den, axis=-1, keepdims=True)[:, None, :]
    o = jnp.broadcast_to(num / den, hs)
    ot_ref[...] = jnp.where(lane[:, None, :] == b, o, ot_ref[...])

    @pl.when(b == pl.num_programs(0) - 1)
    def _():
        o_ref[...] = ot_ref[...].reshape(rows, PAGE_SIZE).T.astype(o_ref.dtype)


def _fox_decode_call(page_table, ck_t, cv_t, clf_t, j, q_t, k_t, v_t, lf_t):
    n_seq, n_pages = page_table.shape
    assert n_seq == PAGE_SIZE, "the sample batch is laid out on the 128 lanes"
    hs = (FOX_HEADS, FOX_HEAD_DIM, PAGE_SIZE)
    kv_page = lambda p: pl.BlockSpec((None, None) + hs, lambda b, pt: (j, pt[b, p], 0, 0, 0))
    lf_page = lambda p: pl.BlockSpec((None, None, FOX_HEADS, PAGE_SIZE), lambda b, pt: (j, pt[b, p], 0, 0))
    whole = lambda shape: pl.BlockSpec(shape, lambda b, pt: (0,) * len(shape))
    pages = range(n_pages)
    return pl.pallas_call(
        _fox_decode_body,
        grid_spec=pltpu.PrefetchScalarGridSpec(
            num_scalar_prefetch=1,
            grid=(n_seq,),
            in_specs=([kv_page(p) for p in pages] + [kv_page(p) for p in pages] + [lf_page(p) for p in pages]
                      + [whole(hs), whole(hs), whole(hs), whole((FOX_HEADS, PAGE_SIZE))]),
            out_specs=whole((n_seq, D_MODEL)),
            scratch_shapes=[pltpu.VMEM(hs, F32)],
        ),
        out_shape=jax.ShapeDtypeStruct((n_seq, D_MODEL), BF16),
        compiler_params=_params("arbitrary"),
        name="fox_decode",
    )(page_table, *([ck_t] * n_pages), *([cv_t] * n_pages), *([clf_t] * n_pages), q_t, k_t, v_t, lf_t)


def _out_ffn_body(x_ref, y_ref, wo_ref, gate1_ref, n2_ref, shift2_ref, scale2_ref, gate2_ref,
                  wg_ref, wu_ref, wd_ref, o_ref):
    mix = jnp.dot(y_ref[...], wo_ref[...], preferred_element_type=F32)
    x1 = x_ref[...] + gate1_ref[...] * mix
    o_ref[...] = _ffn_tail(x1, n2_ref[...], shift2_ref[...], scale2_ref[...], gate2_ref[...], wg_ref, wu_ref, wd_ref)


def _ffn_specs():
    return [_resident((D_MODEL, D_FF)), _resident((D_MODEL, D_FF)), _resident((D_FF, D_MODEL))]


def _out_ffn_call(x, y, wo, n2, rs, ffn):
    return pl.pallas_call(
        _out_ffn_body,
        grid=rs.grid,
        in_specs=[rs.rows(D_MODEL), rs.rows(D_MODEL), _resident((D_MODEL, D_MODEL)), rs.mod(2),
                  _resident((1, D_MODEL)), rs.mod(3), rs.mod(4), rs.mod(5)] + _ffn_specs(),
        out_specs=rs.rows(D_MODEL),
        out_shape=rs.shape(D_MODEL),
        compiler_params=_params("arbitrary", "arbitrary"),
        name="out_ffn",
    )(x, y, wo, rs.mods, n2, rs.mods, rs.mods, rs.mods, *ffn)


def _pool_groups(window_sums, h, inv_cnt, pw_ref, ps_ref):
    outs = []
    for g in range(len(POOL_WINDOWS)):
        sl = slice(g * POOL_GROUP_DIM, (g + 1) * POOL_GROUP_DIM)
        d = (window_sums[g] * inv_cnt[g] - h[:, sl]).astype(BF16)
        outs.append(jnp.dot(d, pw_ref[g], preferred_element_type=F32))
    return jnp.concatenate(outs, axis=-1) * ps_ref[...]


def _pool_ffn_body(x_ref, n1_ref, shift1_ref, scale1_ref, gate1_ref, pw_ref, ps_ref,
                   n2_ref, shift2_ref, scale2_ref, gate2_ref, wg_ref, wu_ref, wd_ref,
                   o_ref, tail_ref, ext_ref):
    tm = x_ref.shape[0]
    i = pl.program_id(1)
    x = x_ref[...]
    h = _prenorm(x, n1_ref[...], shift1_ref[...], scale1_ref[...])

    @pl.when(i == 0)
    def _():
        ext_ref[0:POOL_HALO, :] = jnp.zeros((POOL_HALO, D_MODEL), F32)

    ext_ref[POOL_HALO:, :] = h
    pos = i * tm + lax.broadcasted_iota(jnp.int32, (tm, 1), 0)
    sums, inv_cnt = [], []
    run, back = h, 1
    for g, win in enumerate(POOL_WINDOWS):
        lo = g * POOL_GROUP_DIM
        while back < win:
            run = run + ext_ref[POOL_HALO - back:POOL_HALO - back + tm, lo:]
            back += 1
        sums.append(run[:, :POOL_GROUP_DIM])
        if g + 1 < len(POOL_WINDOWS):
            run = run[:, POOL_GROUP_DIM:]
        inv_cnt.append(1.0 / jnp.minimum(win, pos + 1).astype(F32))
    y = _pool_groups(sums, h, inv_cnt, pw_ref, ps_ref)
    x1 = x + gate1_ref[...] * y
    o_ref[...] = _ffn_tail(x1, n2_ref[...], shift2_ref[...], scale2_ref[...], gate2_ref[...], wg_ref, wu_ref, wd_ref)
    halo = ext_ref[tm:tm + POOL_HALO, :]
    ext_ref[0:POOL_HALO, :] = halo
    tail_ref[...] = halo


def _pool_ffn_call(x, n1, pool_w, pool_scale, n2, rs, ffn):
    g = len(POOL_WINDOWS)
    return pl.pallas_call(
        _pool_ffn_body,
        grid=rs.grid,
        in_specs=[rs.rows(D_MODEL), _resident((1, D_MODEL)), rs.mod(0), rs.mod(1), rs.mod(2),
                  _resident((g, POOL_GROUP_DIM, POOL_GROUP_DIM)), _resident((1, D_MODEL)),
                  _resident((1, D_MODEL)), rs.mod(3), rs.mod(4), rs.mod(5)] + _ffn_specs(),
        out_specs=(rs.rows(D_MODEL), pl.BlockSpec((None, POOL_HALO, D_MODEL), lambda b, i: (b, 0, 0))),
        out_shape=(rs.shape(D_MODEL), jax.ShapeDtypeStruct((rs.batch, POOL_HALO, D_MODEL), F32)),
        scratch_shapes=[pltpu.VMEM((POOL_HALO + rs.tm, D_MODEL), F32)],
        compiler_params=_params("arbitrary", "arbitrary"),
        name="pool_ffn",
    )(x, n1, rs.mods, rs.mods, rs.mods, pool_w, pool_scale, n2, rs.mods, rs.mods, rs.mods, *ffn)


def _pool_sample_body(x_ref, st_ref, n1_ref, shift1_ref, scale1_ref, gate1_ref, pw_ref, ps_ref,
                      n2_ref, shift2_ref, scale2_ref, gate2_ref, wg_ref, wu_ref, wd_ref, o_ref, st_out_ref):
    x = x_ref[...]
    h = _prenorm(x, n1_ref[...], shift1_ref[...], scale1_ref[...])
    sums, inv_cnt = [], []
    run = h
    back = 1
    for g, win in enumerate(POOL_WINDOWS):
        lo = g * POOL_GROUP_DIM
        while back < win:
            run = run + st_ref[POOL_STATE - back]
            back += 1
        sums.append(run[:, lo:lo + POOL_GROUP_DIM])
        inv_cnt.append(1.0 / win)
    y = _pool_groups(sums, h, inv_cnt, pw_ref, ps_ref)
    x1 = x + gate1_ref[...] * y
    o_ref[...] = _ffn_tail(x1, n2_ref[...], shift2_ref[...], scale2_ref[...], gate2_ref[...], wg_ref, wu_ref, wd_ref)
    st_out_ref[0:POOL_STATE - 1] = st_ref[1:POOL_STATE]
    st_out_ref[POOL_STATE - 1] = h


def _pool_sample_call(x, state_t, n1, pool_w, pool_scale, n2, mods, ffn):
    bd = x.shape[0]
    g = len(POOL_WINDOWS)
    mod = lambda c: pl.BlockSpec((bd, D_MODEL), lambda i: (0, c), pipeline_mode=pl.Buffered(1))
    return pl.pallas_call(
        _pool_sample_body,
        grid=(1,),
        in_specs=[_resident((bd, D_MODEL)), _resident((POOL_STATE, bd, D_MODEL)), _resident((1, D_MODEL)),
                  mod(0), mod(1), mod(2), _resident((g, POOL_GROUP_DIM, POOL_GROUP_DIM)), _resident((1, D_MODEL)),
                  _resident((1, D_MODEL)), mod(3), mod(4), mod(5)] + _ffn_specs(),
        out_specs=(pl.BlockSpec((bd, D_MODEL), lambda i: (0, 0)),
                   pl.BlockSpec((POOL_STATE, bd, D_MODEL), lambda i: (0, 0, 0))),
        out_shape=(jax.ShapeDtypeStruct((bd, D_MODEL), F32), jax.ShapeDtypeStruct((POOL_STATE, bd, D_MODEL), F32)),
        compiler_params=_params("arbitrary"),
        name="pool_sample",
    )(x, state_t, n1, mods, mods, mods, pool_w, pool_scale, n2, mods, mods, mods, *ffn)


def _gla_gate(hb, wa1_ref, wa2_ref, ba):
    a1 = jnp.dot(hb, wa1_ref[...], preferred_element_type=F32).astype(BF16)
    return _log_sigmoid(jnp.dot(a1, wa2_ref[...], preferred_element_type=F32) + ba) / GLA_TAU


def _gla_proj_body(x_ref, n1_ref, shift_ref, scale_ref, wq_ref, wk_ref, wv_ref, wa1_ref, wa2_ref, ba_ref, wr_ref,
                   q_ref, k_ref, v_ref, g_ref, r_ref):
    hb = _prenorm(x_ref[...], n1_ref[...], shift_ref[...], scale_ref[...]).astype(BF16)
    q_ref[...] = jnp.dot(hb, wq_ref[...], preferred_element_type=F32) * (GLA_DK ** -0.5)
    k_ref[...] = jnp.dot(hb, wk_ref[...], preferred_element_type=F32)
    v_ref[...] = jnp.dot(hb, wv_ref[...], preferred_element_type=F32).astype(v_ref.dtype)
    g_ref[...] = _gla_gate(hb, wa1_ref, wa2_ref, ba_ref[...])
    r_ref[...] = _silu(jnp.dot(hb, wr_ref[...], preferred_element_type=F32))


def _gla_proj_call(x, n1, rs, w):
    kw = GLA_HEADS * GLA_DK
    rank = w["wa1"].shape[1]
    return pl.pallas_call(
        _gla_proj_body,
        grid=rs.grid,
        in_specs=[rs.rows(D_MODEL), _resident((1, D_MODEL)), rs.mod(0), rs.mod(1),
                  _resident((D_MODEL, kw)), _resident((D_MODEL, kw)), _resident((D_MODEL, D_MODEL)),
                  _resident((D_MODEL, rank)), _resident((rank, kw)), _resident((1, kw)), _resident((D_MODEL, D_MODEL))],
        out_specs=(rs.rows(kw), rs.rows(kw), rs.rows(D_MODEL), rs.rows(kw), rs.rows(D_MODEL)),
        out_shape=(rs.shape(kw), rs.shape(kw), rs.shape(D_MODEL, BF16), rs.shape(kw), rs.shape(D_MODEL)),
        compiler_params=_params("arbitrary", "arbitrary"),
        name="gla_proj",
    )(x, n1, rs.mods, rs.mods, w["wq"], w["wk"], w["wv"], w["wa1"], w["wa2"], w["ba"], w["wr"])


def _gla_out_norm(o, gn, r):
    return _rms(o, gn) * r


def _gla_scan_body(q_ref, k_ref, v_ref, g_ref, r_ref, gn_ref, y_ref, s_out_ref, st_ref):
    batch, rows = q_ref.shape[0], q_ref.shape[1]
    c = GLA_CHUNK

    @pl.when(pl.program_id(0) == 0)
    def _():
        st_ref[...] = jnp.zeros_like(st_ref)

    ri = lax.broadcasted_iota(jnp.int32, (c, c), 0)
    ci = lax.broadcasted_iota(jnp.int32, (c, c), 1)
    causal = ci <= ri
    tril = causal.astype(BF16)
    gn = gn_ref[...]

    def chunk(n, carry):
        r0 = pl.multiple_of(n * c, c)
        for b in range(batch):
            q, k, g = (ref[b, pl.ds(r0, c), :] for ref in (q_ref, k_ref, g_ref))
            v, rr = v_ref[b, pl.ds(r0, c), :], r_ref[b, pl.ds(r0, c), :]
            bcum = _dot3_left(tril, g)
            b_last = bcum[c - 1:c, :]
            e = jnp.exp(bcum)
            qe = (q * e).astype(BF16)
            ke = (k * jnp.exp(-bcum)).astype(BF16)
            kd = (k * jnp.exp(b_last - bcum)).astype(BF16)
            decay = jnp.exp(b_last)
            for h in range(GLA_HEADS):
                ks = slice(h * GLA_DK, (h + 1) * GLA_DK)
                vs = slice(h * GLA_DV, (h + 1) * GLA_DV)
                st = st_ref[b, h]
                a = lax.dot_general(qe[:, ks], ke[:, ks], NT_DIMS, preferred_element_type=F32)
                a = jnp.where(causal, a, 0.0).astype(BF16)
                o = (jnp.dot(a, v[:, vs], preferred_element_type=F32)
                     + lax.dot_general(qe[:, ks], st.astype(BF16), NT_DIMS, preferred_element_type=F32))
                st_ref[b, h] = st * decay[:, ks] + lax.dot_general(v[:, vs], kd[:, ks], TN_DIMS,
                                                                  preferred_element_type=F32)
                y_ref[b, pl.ds(r0, c), vs] = _gla_out_norm(o, gn, rr[:, vs]).astype(BF16)
        return carry

    lax.fori_loop(0, rows // c, chunk, 0)

    @pl.when(pl.program_id(0) == pl.num_programs(0) - 1)
    def _():
        for b in range(batch):
            for h in range(GLA_HEADS):
                s_out_ref[b, h] = st_ref[b, h].T


def _gla_scan_call(q, k, v, g, r, gn, rows):
    batch, length, kw = q.shape
    blk = lambda cols: pl.BlockSpec((batch, rows, cols), lambda i: (0, i, 0))
    s_shape = (batch, GLA_HEADS, GLA_DK, GLA_DV)
    return pl.pallas_call(
        _gla_scan_body,
        grid=(length // rows,),
        in_specs=[blk(kw), blk(kw), blk(D_MODEL), blk(kw), blk(D_MODEL), _resident((1, GLA_DV))],
        out_specs=(blk(D_MODEL), pl.BlockSpec(s_shape, lambda i: (0, 0, 0, 0))),
        out_shape=(jax.ShapeDtypeStruct((batch, length, D_MODEL), BF16), jax.ShapeDtypeStruct(s_shape, F32)),
        scratch_shapes=[pltpu.VMEM((batch, GLA_HEADS, GLA_DV, GLA_DK), F32)],
        compiler_params=_params("arbitrary"),
        name="gla_scan",
    )(q, k, v, g, r, gn)


def _gla_sample_proj_body(x_ref, n1_ref, shift_ref, scale_ref, wq_ref, wk_ref, wv_ref, wa1_ref, wa2_ref, ba_ref,
                          wr_ref, q_ref, k_ref, v_ref, g_ref, r_ref):
    hb = _prenorm(x_ref[...], n1_ref[...], shift_ref[...], scale_ref[...]).astype(BF16)
    q_ref[...] = jnp.dot(hb, wq_ref[...], preferred_element_type=F32).T * (GLA_DK ** -0.5)
    k_ref[...] = jnp.dot(hb, wk_ref[...], preferred_element_type=F32).T
    v_ref[...] = jnp.dot(hb, wv_ref[...], preferred_element_type=F32)
    g_ref[...] = _gla_gate(hb, wa1_ref, wa2_ref, ba_ref[...]).T
    r_ref[...] = _silu(jnp.dot(hb, wr_ref[...], preferred_element_type=F32))


def _gla_sample_proj_call(x, n1, mods, w):
    bd = x.shape[0]
    kw = GLA_HEADS * GLA_DK
    rank = w["wa1"].shape[1]
    mod = lambda c: pl.BlockSpec((bd, D_MODEL), lambda i: (0, c), pipeline_mode=pl.Buffered(1))
    full = lambda *s: pl.BlockSpec(s, lambda i: (0,) * len(s))
    return pl.pallas_call(
        _gla_sample_proj_body,
        grid=(1,),
        in_specs=[_resident((bd, D_MODEL)), _resident((1, D_MODEL)), mod(0), mod(1),
                  _resident((D_MODEL, kw)), _resident((D_MODEL, kw)), _resident((D_MODEL, D_MODEL)),
                  _resident((D_MODEL, rank)), _resident((rank, kw)), _resident((1, kw)), _resident((D_MODEL, D_MODEL))],
        out_specs=(full(kw, bd), full(kw, bd), full(bd, D_MODEL), full(kw, bd), full(bd, D_MODEL)),
        out_shape=(jax.ShapeDtypeStruct((kw, bd), F32), jax.ShapeDtypeStruct((kw, bd), F32),
                   jax.ShapeDtypeStruct((bd, D_MODEL), F32), jax.ShapeDtypeStruct((kw, bd), F32),
                   jax.ShapeDtypeStruct((bd, D_MODEL), F32)),
        compiler_params=_params("arbitrary"),
        name="gla_sample_proj",
    )(x, n1, mods, mods, w["wq"], w["wk"], w["wv"], w["wa1"], w["wa2"], w["ba"], w["wr"])


def _gla_decode_body(q_ref, k_ref, g_ref, v_ref, r_ref, gn_ref, s_ref, y_ref, s_out_ref):
    b = pl.program_id(0)
    n = q_ref.shape[1]
    sel = (lax.broadcasted_iota(jnp.int32, (n, n), 0) == b).astype(BF16)
    wide = lambda col: jnp.concatenate([col, col], axis=-1).reshape(GLA_HEADS, GLA_DK, GLA_DV)
    qcol = wide(_lane_column(q_ref[...], sel))
    kcol = wide(_lane_column(k_ref[...], sel))
    decay = wide(jnp.exp(_lane_column(g_ref[...], sel)))
    gn = gn_ref[...]
    for h in range(GLA_HEADS):
        vs = slice(h * GLA_DV, (h + 1) * GLA_DV)
        s_new = s_ref[h] * decay[h] + kcol[h] * v_ref[:, vs]
        s_out_ref[h] = s_new
        o = jnp.sum(qcol[h] * s_new, axis=0, keepdims=True)
        y_ref[:, vs] = _gla_out_norm(o, gn, r_ref[:, vs])


def _gla_decode_call(q_t, k_t, g_t, v, r, gn, state):
    kw, bd = q_t.shape
    assert bd == V7X_LANES, "the sample batch is laid out on the 128 lanes"
    full = lambda *s: pl.BlockSpec(s, lambda i: (0,) * len(s))
    row = pl.BlockSpec((None, 1, D_MODEL), lambda i: (i, 0, 0))
    st = pl.BlockSpec((None, GLA_HEADS, GLA_DK, GLA_DV), lambda i: (i, 0, 0, 0))
    return pl.pallas_call(
        _gla_decode_body,
        grid=(bd,),
        in_specs=[full(kw, bd), full(kw, bd), full(kw, bd), row, row, full(1, GLA_DV), st],
        out_specs=(row, st),
        out_shape=(jax.ShapeDtypeStruct((bd, 1, D_MODEL), F32), jax.ShapeDtypeStruct(state.shape, F32)),
        compiler_params=_params("arbitrary"),
        name="gla_decode",
    )(q_t, k_t, g_t, v, r, gn, state)


def kernel(x_prompt, x_sample, cache_k, cache_v, cache_logf, state_pool, state_gla, page_table, c_prompt, c_sample, norm1, norm2, ada_w, ada_b, fox_wq, fox_wk, fox_wv, fox_wf, fox_bf, fox_qn, fox_kn, fox_wo, pool_w, pool_scale, gla_wq, gla_wk, gla_wv, gla_wa1, gla_wa2, gla_ba, gla_wr, gla_gn, gla_wo, ffn_wg, ffn_wu, ffn_wd):
    batch, length, d = x_prompt.shape
    bd = x_sample.shape[0]
    depth = norm1.shape[0]
    assert d == D_MODEL and x_sample.shape[1] == 1

    c_all = jnp.concatenate([c_prompt, c_sample], axis=0)
    c_all = jnp.pad(c_all, ((0, -c_all.shape[0] % 8), (0, 0)))
    mods = _ada_call(c_all, ada_w, ada_b)

    ck_t = jnp.transpose(cache_k, (0, 1, 3, 4, 2))
    cv_t = jnp.transpose(cache_v, (0, 1, 3, 4, 2))
    clf_t = jnp.transpose(cache_logf, (0, 1, 3, 2))

    xp = x_prompt
    xs = x_sample.reshape(1, bd, d)
    outs = {k: [] for k in ("k_p", "v_p", "lf_p", "k_s", "v_s", "lf_s", "pool_p", "pool_s", "gla_p", "gla_s")}
    for i in range(depth):
        kind, j = i % N_MIXERS, i // N_MIXERS
        rp = _Rows(mods[i, :batch], batch, length, ROW_BLOCK)
        rsm = _Rows(mods[i, batch:batch + bd], 1, bd, bd, per_row=True)
        n1, n2 = norm1[i][None], norm2[i][None]
        ffn = (ffn_wg[i].astype(BF16), ffn_wu[i].astype(BF16), ffn_wd[i].astype(BF16))
        if kind == 0:
            w = dict(wq_t=fox_wq[j].T.astype(BF16), wk_t=fox_wk[j].T.astype(BF16), wv_t=fox_wv[j].T.astype(BF16),
                     wf_t=fox_wf[j].T.astype(BF16), bf=fox_bf[j][:, None], qn=fox_qn[j][:, None], kn=fox_kn[j][:, None])
            wo = fox_wo[j].astype(BF16)
            qa, kr, k_t, v_t, lf_t = _fox_proj_call(xp, n1, rp, w, True)
            yp = _fox_attn_call(qa, kr, v_t)
            qs_t, ks_t, vs_t, lfs_t = _fox_proj_call(xs, n1, rsm, w, False)
            ys = _fox_decode_call(page_table, ck_t, cv_t, clf_t, j, qs_t[0], ks_t[0], vs_t[0], lfs_t[0])
            outs["k_p"].append(jnp.transpose(k_t, (0, 3, 1, 2)))
            outs["v_p"].append(jnp.transpose(v_t, (0, 3, 1, 2)))
            outs["lf_p"].append(jnp.transpose(lf_t, (0, 2, 1)))
            outs["k_s"].append(jnp.transpose(ks_t, (3, 0, 1, 2)))
            outs["v_s"].append(jnp.transpose(vs_t, (3, 0, 1, 2)))
            outs["lf_s"].append(jnp.transpose(lfs_t, (2, 0, 1)))
            xp = _out_ffn_call(xp, yp, wo, n2, rp, ffn)
            xs = _out_ffn_call(xs, ys[None], wo, n2, rsm, ffn)
        elif kind == 1:
            pw, ps = pool_w[j].astype(BF16), pool_scale[j][None]
            xp, tail = _pool_ffn_call(xp, n1, pw, ps, n2, _Rows(mods[i, :batch], batch, length, POOL_ROW_BLOCK), ffn)
            st_t = jnp.transpose(state_pool[j], (1, 0, 2))
            xs2, st_new = _pool_sample_call(xs[0], st_t, n1, pw, ps, n2, rsm.mods, ffn)
            xs = xs2[None]
            outs["pool_p"].append(tail[:, POOL_HALO - POOL_STATE:])
            outs["pool_s"].append(jnp.transpose(st_new, (1, 0, 2)))
        else:
            w = dict(wq=gla_wq[j].astype(BF16), wk=gla_wk[j].astype(BF16), wv=gla_wv[j].astype(BF16),
                     wa1=gla_wa1[j].astype(BF16), wa2=gla_wa2[j].astype(BF16), ba=gla_ba[j][None],
                     wr=gla_wr[j].astype(BF16))
            wo, gn = gla_wo[j].astype(BF16), gla_gn[j][None]
            q, k, v, g, r = _gla_proj_call(xp, n1, rp, w)
            yp, s_fin = _gla_scan_call(q, k, v, g, r, gn, 4 * GLA_CHUNK)
            q_t, k_t, vs_, g_t, r_s = _gla_sample_proj_call(xs[0], n1, rsm.mods, w)
            ys, s_new = _gla_decode_call(q_t, k_t, g_t, vs_[:, None], r_s[:, None], gn, state_gla[j])
            outs["gla_p"].append(s_fin)
            outs["gla_s"].append(s_new)
            xp = _out_ffn_call(xp, yp, wo, n2, rp, ffn)
            xs = _out_ffn_call(xs, ys.reshape(1, bd, d).astype(BF16), wo, n2, rsm, ffn)

    stack = lambda name: jnp.stack(outs[name])
    return (xp, xs.reshape(bd, 1, d),
            stack("k_p"), stack("v_p"), stack("lf_p"),
            stack("k_s"), stack("v_s"), stack("lf_s"),
            stack("pool_p"), stack("pool_s"), stack("gla_p"), stack("gla_s"))
```

```python
import functools
import math

import jax
import jax.numpy as jnp
import numpy as np
from jax import lax
from jax.experimental import pallas as pl
from jax.experimental.pallas import tpu as pltpu

F32, BF16 = jnp.float32, jnp.bfloat16

D_MODEL = 1024
N_MIXERS = 3
PAGE_SIZE = 128
FOX_HEADS = 16
FOX_HEAD_DIM = 64
FOX_SCALE = FOX_HEAD_DIM ** -0.5
LOG2E = math.log2(math.e)
FOX_AUG_ROWS = 16
POOL_WINDOWS = (2, 4, 8, 16)
POOL_GROUP_DIM = D_MODEL // len(POOL_WINDOWS)
POOL_STATE = max(POOL_WINDOWS) - 1
POOL_HALO = 16
GLA_HEADS = 4
GLA_DK = 128
GLA_DV = 256
GLA_TAU = 16.0
GLA_CHUNK = 64
D_FF = 2816
EPS = 1e-6

V7X_LANES = 128
V7X_VMEM_LIMIT_BYTES = 56 * 1024 * 1024

ROW_BLOCK = 512
POOL_ROW_BLOCK = 256
ATTN_BLOCK = 512
ATTN_HEADS = 8
V7X_MXU_DIM = 256
FF_CHUNK = 6 * V7X_MXU_DIM
ADA_COLS = 1536

NT_DIMS = (((1,), (1,)), ((), ()))
TN_DIMS = (((0,), (0,)), ((), ()))


def _params(*sem):
    return pltpu.CompilerParams(dimension_semantics=sem, vmem_limit_bytes=V7X_VMEM_LIMIT_BYTES)


def _resident(shape):
    zeros = (0,) * len(shape)
    return pl.BlockSpec(shape, lambda *_: zeros, pipeline_mode=pl.Buffered(1))


def _silu(x):
    return x * jax.nn.sigmoid(x)


def _log_sigmoid(x):
    return jnp.minimum(x, 0.0) - jnp.log1p(jnp.exp(-jnp.abs(x)))


def _rms(x, g):
    return x * lax.rsqrt(jnp.mean(x * x, axis=-1, keepdims=True) + EPS) * g


def _prenorm(x, g, shift, scale):
    return _rms(x, g) * (1.0 + scale) + shift


def _split3(x):
    hi = x.astype(BF16)
    r1 = x - hi.astype(F32)
    mid = r1.astype(BF16)
    lo = (r1 - mid.astype(F32)).astype(BF16)
    return hi, mid, lo


def _dot3(x, m01):
    hi, mid, lo = _split3(x)
    return (jnp.dot(hi, m01, preferred_element_type=F32)
            + jnp.dot(mid, m01, preferred_element_type=F32)
            + jnp.dot(lo, m01, preferred_element_type=F32))


def _dot3_left(m01, x):
    hi, mid, lo = _split3(x)
    return (jnp.dot(m01, hi, preferred_element_type=F32)
            + jnp.dot(m01, mid, preferred_element_type=F32)
            + jnp.dot(m01, lo, preferred_element_type=F32))


def _ffn_tail(x1, n2, shift2, scale2, gate2, wg_ref, wu_ref, wd_ref):
    f = _prenorm(x1, n2, shift2, scale2).astype(BF16)
    acc = jnp.zeros(x1.shape, F32)
    for c0 in range(0, D_FF, FF_CHUNK):
        c1 = min(c0 + FF_CHUNK, D_FF)
        g = jnp.dot(f, wg_ref[:, c0:c1], preferred_element_type=F32)
        u = jnp.dot(f, wu_ref[:, c0:c1], preferred_element_type=F32)
        a = (_silu(g) * u).astype(BF16)
        acc = acc + jnp.dot(a, wd_ref[c0:c1, :], preferred_element_type=F32)
    return x1 + gate2 * acc


def _ada_body(c_ref, w_ref, b_ref, o_ref):
    a = _silu(c_ref[...]).astype(BF16)
    o_ref[...] = jnp.dot(a, w_ref[...].astype(BF16), preferred_element_type=F32) + b_ref[...]


def _ada_call(c_all, ada_w, ada_b):
    depth, d, n = ada_w.shape
    rows = c_all.shape[0]
    return pl.pallas_call(
        _ada_body,
        grid=(depth, n // ADA_COLS),
        in_specs=[
            pl.BlockSpec((rows, d), lambda i, j: (0, 0)),
            pl.BlockSpec((None, d, ADA_COLS), lambda i, j: (i, 0, j)),
            pl.BlockSpec((None, 1, ADA_COLS), lambda i, j: (i, 0, j)),
        ],
        out_specs=pl.BlockSpec((None, rows, ADA_COLS), lambda i, j: (i, 0, j)),
        out_shape=jax.ShapeDtypeStruct((depth, rows, n), F32),
        compiler_params=_params("arbitrary", "arbitrary"),
        name="ada_mods",
    )(c_all, ada_w, ada_b.reshape(depth, 1, n))


class _Rows:
    def __init__(self, mods, batch, length, tm, per_row=False):
        self.batch, self.length, self.tm = batch, length, tm
        self.grid = (batch, length // tm)
        self.per_row = per_row
        self.mods = mods if per_row else mods.reshape(batch, 6, 1, D_MODEL)

    def rows(self, cols):
        return pl.BlockSpec((None, self.tm, cols), lambda b, i: (b, i, 0))

    def mod(self, c):
        if self.per_row:
            return pl.BlockSpec((self.tm, D_MODEL), lambda b, i: (i, c))
        return pl.BlockSpec((None, None, 1, D_MODEL), lambda b, i: (b, c, 0, 0))

    def shape(self, cols, dtype=F32):
        return jax.ShapeDtypeStruct((self.batch, self.length, cols), dtype)


def _head_norm(y, gain_col):
    ms = jnp.mean(y * y, axis=1, keepdims=True)
    return y * lax.rsqrt(ms + EPS) * gain_col[None, :, :]


FOX_PROJ_INPUTS = 12


def _fox_proj_body(*refs, with_bias_rows):
    _fox_proj_compute(pl.program_id(1) == 0, *refs, with_bias_rows=with_bias_rows)


def _fox_proj_compute(first_block, x_ref, n1_ref, shift_ref, scale_ref, wq_ref, wk_ref, wv_ref, wf_ref, bf_ref,
                      qn_ref, kn_ref, tri_ref, *out_refs, with_bias_rows):
    tm = x_ref.shape[0]
    hb = _prenorm(x_ref[...], n1_ref[...], shift_ref[...], scale_ref[...]).astype(BF16)

    def proj_t(w_ref):
        return lax.dot_general(w_ref[...], hb, NT_DIMS, preferred_element_type=F32)

    q = _head_norm(proj_t(wq_ref).reshape(FOX_HEADS, FOX_HEAD_DIM, tm), qn_ref[...])
    k = _head_norm(proj_t(wk_ref).reshape(FOX_HEADS, FOX_HEAD_DIM, tm), kn_ref[...])
    v = proj_t(wv_ref).reshape(FOX_HEADS, FOX_HEAD_DIM, tm)
    lf = _log_sigmoid(proj_t(wf_ref) + bf_ref[...])

    if not with_bias_rows:
        q_ref, k_ref, v_ref, lf_ref = out_refs
        q_ref[...], k_ref[...], v_ref[...], lf_ref[...] = q, k, v, lf
        return

    qa_ref, kr_ref, k_ref, v_ref, lf_ref, carry_ref = out_refs
    k_ref[...], v_ref[...], lf_ref[...] = k, v, lf

    @pl.when(first_block)
    def _():
        carry_ref[...] = jnp.zeros_like(carry_ref)

    f_cum = _dot3(lf, tri_ref[...]) + carry_ref[:, 0:1]
    carry_ref[...] = jnp.broadcast_to(f_cum[:, tm - 1:tm], carry_ref.shape)

    hi, mid, lo = (p.astype(F32)[:, None, :] for p in _split3(f_cum * LOG2E))
    r = lax.broadcasted_iota(jnp.int32, (FOX_HEADS, FOX_AUG_ROWS, tm), 1)
    q_rows = jnp.where(r == 0, hi, jnp.where(r == 1, mid, jnp.where(r == 2, lo, jnp.where(r < 6, 1.0, 0.0))))
    k_rows = jnp.where(r < 3, 1.0, jnp.where(r == 3, -hi, jnp.where(r == 4, -mid, jnp.where(r == 5, -lo, 0.0))))
    pad =jnp.zeros((FOX_HEADS, 2 * FOX_HEAD_DIM - FOX_HEAD_DIM - FOX_AUG_ROWS, tm), BF16)
    qa_ref[:, 0:FOX_HEAD_DIM, :] = (q * (FOX_SCALE * LOG2E)).astype(BF16)
    qa_ref[:, FOX_HEAD_DIM:FOX_HEAD_DIM + FOX_AUG_ROWS, :] = q_rows.astype(BF16)
    qa_ref[:, FOX_HEAD_DIM + FOX_AUG_ROWS:, :] = pad
    ka = jnp.concatenate([k.astype(BF16), k_rows.astype(BF16), pad], axis=1)
    for h in range(FOX_HEADS):
        kr_ref[h] = ka[h].T


def _fox_sample_proj_call(x, n1, rs, w):
    batch, length, tm = rs.batch, rs.length, rs.tm
    hshape = jax.ShapeDtypeStruct((batch, FOX_HEADS, FOX_HEAD_DIM, length), F32)
    hspec = pl.BlockSpec((None, FOX_HEADS, FOX_HEAD_DIM, tm), lambda b, i: (b, 0, 0, i))
    tri = jnp.triu(jnp.ones((tm, tm), BF16))
    return pl.pallas_call(
        functools.partial(_fox_proj_body, with_bias_rows=False),
        grid=rs.grid,
        in_specs=[rs.rows(D_MODEL), _resident((1, D_MODEL)), rs.mod(0), rs.mod(1),
                  _resident((D_MODEL, D_MODEL)), _resident((D_MODEL, D_MODEL)), _resident((D_MODEL, D_MODEL)),
                  _resident((FOX_HEADS, D_MODEL)), _resident((FOX_HEADS, 1)),
                  _resident((FOX_HEAD_DIM, 1)), _resident((FOX_HEAD_DIM, 1)), _resident((tm, tm))],
        out_specs=(hspec,) * 3 + (pl.BlockSpec((None, FOX_HEADS, tm), lambda b, i: (b, 0, i)),),
        out_shape=(hshape,) * 3 + (jax.ShapeDtypeStruct((batch, FOX_HEADS, length), F32),),
        compiler_params=_params("arbitrary", "arbitrary"),
        name="fox_proj",
    )(x, n1, rs.mods, rs.mods, w["wq_t"], w["wk_t"], w["wv_t"], w["wf_t"], w["bf"], w["qn"], w["kn"], tri)


def _fox_attn_body(qi_tab, ki_tab, q_ref, k_ref, v_ref, o_ref, m_ref, acc_ref):
    t = pl.program_id(2)
    qi, ki = qi_tab[t], ki_tab[t]
    nh, tk, tq = k_ref.shape[0], k_ref.shape[1], q_ref.shape[2]
    hd = FOX_HEAD_DIM

    @pl.when(ki == 0)
    def _():
        m_ref[...] = jnp.full_like(m_ref, -jnp.inf)
        acc_ref[...] = jnp.zeros_like(acc_ref)

    def step(diagonal):
        scores = [jnp.dot(k_ref[hh], q_ref[hh], preferred_element_type=F32) for hh in range(nh)]
        ones_row = (lax.broadcasted_iota(jnp.int32, (hd, tk), 0) == 0).astype(BF16)
        for hh in range(nh):
            s = scores[hh]
            if diagonal:
                key = lax.broadcasted_iota(jnp.int32, (tk, tq), 0)
                qry = lax.broadcasted_iota(jnp.int32, (tk, tq), 1)
                s = jnp.where(key > qry, -jnp.inf, s)
            m_prev = m_ref[hh]
            m_new = jnp.maximum(m_prev, jnp.max(s, axis=0, keepdims=True))
            p = jnp.exp2(s - m_new).astype(BF16)
            m_ref[hh] = m_new
            vb = v_ref[hh * hd:(hh + 1) * hd, :].astype(BF16)
            v_aug = jnp.concatenate([vb, ones_row] if hh % 2 == 0 else [ones_row, vb], axis=0)
            acc_ref[hh] = (acc_ref[hh] * jnp.exp2(m_prev - m_new)
                           + jnp.dot(v_aug, p, preferred_element_type=F32))

    @pl.when(ki < qi)
    def _():
        step(False)

    @pl.when(ki == qi)
    def _():
        step(True)
        first = lax.broadcasted_iota(jnp.int32, (2 * hd, tq), 0) < hd
        for pair in range(nh // 2):
            a0, a1 = acc_ref[2 * pair], acc_ref[2 * pair + 1]
            o_t = jnp.where(first, a0 / a0[hd:hd + 1, :], a1 / a1[0:1, :])
            o_ref[:, pair * 2 * hd:(pair + 1) * 2 * hd] = o_t.T.astype(o_ref.dtype)


def _fox_attn_call(qa, kr, v_stack, layer):
    batch, _, rows, length = qa.shape
    tq = tk = ATTN_BLOCK
    nh = ATTN_HEADS
    groups = FOX_HEADS // nh
    nq = length // tq
    pairs = [(i, j) for i in range(nq) for j in range(i + 1)]
    qi_tab = jnp.asarray([p[0] for p in pairs], jnp.int32)
    ki_tab = jnp.asarray([p[1] for p in pairs], jnp.int32)
    qa = qa.reshape(batch, groups, nh, rows, length)
    kr = kr.reshape(batch, groups, nh, length, rows)
    v_stack = v_stack.reshape(v_stack.shape[0], batch, groups, nh * FOX_HEAD_DIM, length)
    return pl.pallas_call(
        _fox_attn_body,
        grid_spec=pltpu.PrefetchScalarGridSpec(
            num_scalar_prefetch=2,
            grid=(batch, groups, len(pairs)),
            in_specs=[
                pl.BlockSpec((None, None, nh, rows, tq), lambda b, h, t, qt, kt: (b, h, 0, 0, qt[t])),
                pl.BlockSpec((None, None, nh, tk, rows), lambda b, h, t, qt, kt: (b, h, 0, kt[t], 0)),
                pl.BlockSpec((None, None, None, nh * FOX_HEAD_DIM, tk), lambda b, h, t, qt, kt: (layer, b, h, 0, kt[t])),
            ],
            out_specs=pl.BlockSpec((None, tq, nh * FOX_HEAD_DIM), lambda b, h, t, qt, kt: (b, qt[t], h)),
            scratch_shapes=[pltpu.VMEM((nh, 1, tq), F32), pltpu.VMEM((nh, rows, tq), F32)],
        ),
        out_shape=jax.ShapeDtypeStruct((batch, length, D_MODEL), BF16),
        compiler_params=_params("arbitrary", "arbitrary", "arbitrary"),
        name="fox_attn",
    )(qi_tab, ki_tab, qa, kr, v_stack)


def _lane_column(x_t, sel):
    return _dot3(x_t, sel)


def _fox_decode_compute(*refs):
    n_pages = (len(refs) - 6) // 3
    ck_refs, cv_refs, clf_refs = refs[:n_pages], refs[n_pages:2 * n_pages], refs[2 * n_pages:3 * n_pages]
    q_ref, k_ref, v_ref, lf_ref, o_ref, ot_ref = refs[3 * n_pages:]
    b = pl.program_id(0)
    hs = (FOX_HEADS, FOX_HEAD_DIM, PAGE_SIZE)
    rows = FOX_HEADS * FOX_HEAD_DIM
    lane = lax.broadcasted_iota(jnp.int32, (FOX_HEADS, PAGE_SIZE), 1)

    @pl.when(b == 0)
    def _():
        ot_ref[...] = jnp.zeros_like(ot_ref)

    sel = (lax.broadcasted_iota(jnp.int32, (PAGE_SIZE, PAGE_SIZE), 0) == b).astype(BF16)
    qcol = _lane_column(q_ref[...].reshape(rows, PAGE_SIZE), sel).reshape(hs) * FOX_SCALE
    kcol = _lane_column(k_ref[...].reshape(rows, PAGE_SIZE), sel).reshape(hs)
    vcol = _lane_column(v_ref[...].reshape(rows, PAGE_SIZE), sel).reshape(hs)
    s_self = jnp.sum(qcol * kcol, axis=1)

    lf_all = jnp.concatenate([r[...] for r in clf_refs], axis=0)
    after = (lax.broadcasted_iota(jnp.int32, (PAGE_SIZE, PAGE_SIZE), 0)
             > lax.broadcasted_iota(jnp.int32, (PAGE_SIZE, PAGE_SIZE), 1)).astype(BF16)
    suffix_all = _dot3(lf_all, after)
    later = _lane_column(lf_ref[...], sel)
    scores = [None] * n_pages
    for p in reversed(range(n_pages)):
        pg = slice(p * FOX_HEADS, (p + 1) * FOX_HEADS)
        scores[p] = jnp.sum(qcol * ck_refs[p][...], axis=1) + later + suffix_all[pg]
        later = later + jnp.broadcast_to(suffix_all[pg, 0:1] + lf_all[pg, 0:1], lane.shape)

    m = s_self
    for s in scores:
        m = jnp.maximum(m, s)
    m = jnp.max(m, axis=-1, keepdims=True)
    p_self = jnp.exp(s_self - m)
    den = jnp.where(lane == 0, p_self, 0.0)
    acc = jnp.where(lane[:, None, :] == 0, vcol * p_self[:, None, :], 0.0)
    for p in range(n_pages):
        pr = jnp.exp(scores[p] - m)
        den = den + pr
        acc = acc + cv_refs[p][...] * pr[:, None, :]
    num = jnp.sum(acc, axis=-1, keepdims=True)
    den = jnp.sum(den, axis=-1, keepdims=True)[:, None, :]
    o = jnp.broadcast_to(num / den, hs)
    ot_ref[...] = jnp.where(lane[:, None, :] == b, o, ot_ref[...])

    @pl.when(b == pl.num_programs(0) - 1)
    def _():
        o_ref[...] = ot_ref[...].reshape(rows, PAGE_SIZE).T.astype(o_ref.dtype)


def _fox_front_body(pt_ref, *refs, n_decode_inputs, blocks_per_seq):
    n_in = FOX_PROJ_INPUTS + 2
    proj_in, decode_in = refs[:FOX_PROJ_INPUTS], refs[n_in:n_in + n_decode_inputs]
    qa_ref, kr_ref, k_ref, v_ref, lf_ref, o_ref, carry_ref, ot_ref = refs[n_in + n_decode_inputs:]
    first_block = pl.program_id(0) % blocks_per_seq == 0
    _fox_proj_compute(first_block, *proj_in, qa_ref, kr_ref, k_ref, v_ref, lf_ref, carry_ref, with_bias_rows=True)
    _fox_decode_compute(*decode_in, o_ref, ot_ref)


def _fox_front_call(x, n1, mods, w, layer, stacks, page_table, ck_t, cv_t, clf_t, q_t, k_t, v_t, lf_t):
    batch, length, _ = x.shape
    n_seq, n_pages = page_table.shape
    assert n_seq == PAGE_SIZE, "the sample batch is laid out on the 128 lanes"
    tm = batch * length // n_seq
    assert length % tm == 0 and tm % V7X_LANES == 0
    nblk = length // tm
    mods = mods.reshape(batch, 6, 1, D_MODEL)
    hs = (FOX_HEADS, FOX_HEAD_DIM, PAGE_SIZE)
    pair_rows = 2 * FOX_HEAD_DIM
    whole = lambda shape: pl.BlockSpec(shape, lambda i, pt: (0,) * len(shape), pipeline_mode=pl.Buffered(1))
    mod = lambda c: pl.BlockSpec((None, None, 1, D_MODEL), lambda i, pt: (i // nblk, c, 0, 0))
    kv_page = lambda p: pl.BlockSpec((None, None) + hs, lambda i, pt: (layer, pt[i, p], 0, 0, 0))
    lf_page = lambda p: pl.BlockSpec((None, None, FOX_HEADS, PAGE_SIZE), lambda i, pt: (layer, pt[i, p], 0, 0))
    pages = range(n_pages)
    tri = jnp.triu(jnp.ones((tm, tm), BF16))
    proj_operands = (x, n1, mods, mods, w["wq_t"], w["wk_t"], w["wv_t"], w["wf_t"], w["bf"], w["qn"], w["kn"], tri)
    proj_specs = [pl.BlockSpec((None, tm, D_MODEL), lambda i, pt: (i // nblk, i % nblk, 0)), whole((1, D_MODEL)),
                  mod(0), mod(1), whole((D_MODEL, D_MODEL)), whole((D_MODEL, D_MODEL)), whole((D_MODEL, D_MODEL)),
                  whole((FOX_HEADS, D_MODEL)), whole((FOX_HEADS, 1)), whole((FOX_HEAD_DIM, 1)),
                  whole((FOX_HEAD_DIM, 1)), whole((tm, tm))]
    decode_operands = ([ck_t] * n_pages + [cv_t] * n_pages + [clf_t] * n_pages + [q_t, k_t, v_t, lf_t])
    decode_specs = ([kv_page(p) for p in pages] + [kv_page(p) for p in pages] + [lf_page(p) for p in pages]
                    + [whole(hs), whole(hs), whole(hs), whole((FOX_HEADS, PAGE_SIZE))])
    n_layers = stacks[0].shape[0]
    st_shape = jax.ShapeDtypeStruct((n_layers, batch, FOX_HEADS, FOX_HEAD_DIM, length), F32)
    st_spec = pl.BlockSpec((None, None, FOX_HEADS, FOX_HEAD_DIM, tm), lambda i, pt: (layer, i // nblk, 0, 0, i % nblk))
    n_proj = len(proj_operands)
    return pl.pallas_call(
        functools.partial(_fox_front_body, n_decode_inputs=len(decode_operands), blocks_per_seq=nblk),
        grid_spec=pltpu.PrefetchScalarGridSpec(
            num_scalar_prefetch=1,
            grid=(n_seq,),
            in_specs=proj_specs + [pl.BlockSpec(memory_space=pl.ANY)] * 2 + decode_specs,
            out_specs=(
                pl.BlockSpec((None, FOX_HEADS, pair_rows, tm), lambda i, pt: (i // nblk, 0, 0, i % nblk)),
                pl.BlockSpec((None, FOX_HEADS, tm, pair_rows), lambda i, pt: (i // nblk, 0, i % nblk, 0)),
                st_spec, st_spec,
                pl.BlockSpec((None, FOX_HEADS, tm), lambda i, pt: (i // nblk, 0, i % nblk)),
                pl.BlockSpec((n_seq, D_MODEL), lambda i, pt: (0, 0)),
            ),
            scratch_shapes=[pltpu.VMEM((FOX_HEADS, V7X_LANES), F32), pltpu.VMEM(hs, F32)],
        ),
        out_shape=(
            jax.ShapeDtypeStruct((batch, FOX_HEADS, pair_rows, length), BF16),
            jax.ShapeDtypeStruct((batch, FOX_HEADS, length, pair_rows), BF16),
            st_shape, st_shape,
            jax.ShapeDtypeStruct((batch, FOX_HEADS, length), F32),
            jax.ShapeDtypeStruct((n_seq, D_MODEL), BF16),
        ),
        input_output_aliases={1 + n_proj: 2, 2 + n_proj: 3},
        compiler_params=_params("arbitrary"),
        name="fox_front",
    )(page_table, *proj_operands, *stacks, *decode_operands)


def _out_ffn_body(x_ref, y_ref, wo_ref, gate1_ref, n2_ref, shift2_ref, scale2_ref, gate2_ref,
                  wg_ref, wu_ref, wd_ref, o_ref):
    mix = jnp.dot(y_ref[...], wo_ref[...], preferred_element_type=F32)
    x1 = x_ref[...] + gate1_ref[...] * mix
    o_ref[...] = _ffn_tail(x1, n2_ref[...], shift2_ref[...], scale2_ref[...], gate2_ref[...], wg_ref, wu_ref, wd_ref)


def _ffn_specs():
    return [_resident((D_MODEL, D_FF)), _resident((D_MODEL, D_FF)), _resident((D_FF, D_MODEL))]


def _out_ffn_call(x, y, wo, n2, rs, ffn):
    return pl.pallas_call(
        _out_ffn_body,
        grid=rs.grid,
        in_specs=[rs.rows(D_MODEL), rs.rows(D_MODEL), _resident((D_MODEL, D_MODEL)), rs.mod(2),
                  _resident((1, D_MODEL)), rs.mod(3), rs.mod(4), rs.mod(5)] + _ffn_specs(),
        out_specs=rs.rows(D_MODEL),
        out_shape=rs.shape(D_MODEL),
        compiler_params=_params("arbitrary", "arbitrary"),
        name="out_ffn",
    )(x, y, wo, rs.mods, n2, rs.mods, rs.mods, rs.mods, *ffn)


def _pool_groups(window_sums, h, inv_cnt, pw_ref, ps_ref):
    outs = []
    for g in range(len(POOL_WINDOWS)):
        sl = slice(g * POOL_GROUP_DIM, (g + 1) * POOL_GROUP_DIM)
        d = (window_sums[g] * inv_cnt[g] - h[:, sl]).astype(BF16)
        outs.append(jnp.dot(d, pw_ref[g], preferred_element_type=F32))
    return jnp.concatenate(outs, axis=-1) * ps_ref[...]


def _pool_ffn_body(x_ref, n1_ref, shift1_ref, scale1_ref, gate1_ref, pw_ref, ps_ref,
                   n2_ref, shift2_ref, scale2_ref, gate2_ref, wg_ref, wu_ref, wd_ref,
                   o_ref, tail_ref, ext_ref):
    tm = x_ref.shape[0]
    i = pl.program_id(1)
    x = x_ref[...]
    h = _prenorm(x, n1_ref[...], shift1_ref[...], scale1_ref[...])

    @pl.when(i == 0)
    def _():
        ext_ref[0:POOL_HALO, :] = jnp.zeros((POOL_HALO, D_MODEL), F32)

    ext_ref[POOL_HALO:, :] = h
    pos = i * tm + lax.broadcasted_iota(jnp.int32, (tm, 1), 0)
    sums, inv_cnt = [], []
    run, back = h, 1
    for g, win in enumerate(POOL_WINDOWS):
        lo = g * POOL_GROUP_DIM
        while back < win:
            run = run + ext_ref[POOL_HALO - back:POOL_HALO - back + tm, lo:]
            back += 1
        sums.append(run[:, :POOL_GROUP_DIM])
        if g + 1 < len(POOL_WINDOWS):
            run = run[:, POOL_GROUP_DIM:]
        inv_cnt.append(1.0 / jnp.minimum(win, pos + 1).astype(F32))
    y = _pool_groups(sums, h, inv_cnt, pw_ref, ps_ref)
    x1 = x + gate1_ref[...] * y
    o_ref[...] = _ffn_tail(x1, n2_ref[...], shift2_ref[...], scale2_ref[...], gate2_ref[...], wg_ref, wu_ref, wd_ref)
    halo = ext_ref[tm:tm + POOL_HALO, :]
    ext_ref[0:POOL_HALO, :] = halo
    tail_ref[...] = halo


def _pool_ffn_call(x, n1, pool_w, pool_scale, n2, rs, ffn):
    g = len(POOL_WINDOWS)
    return pl.pallas_call(
        _pool_ffn_body,
        grid=rs.grid,
        in_specs=[rs.rows(D_MODEL), _resident((1, D_MODEL)), rs.mod(0), rs.mod(1), rs.mod(2),
                  _resident((g, POOL_GROUP_DIM, POOL_GROUP_DIM)), _resident((1, D_MODEL)),
                  _resident((1, D_MODEL)), rs.mod(3), rs.mod(4), rs.mod(5)] + _ffn_specs(),
        out_specs=(rs.rows(D_MODEL), pl.BlockSpec((None, POOL_HALO, D_MODEL), lambda b, i: (b, 0, 0))),
        out_shape=(rs.shape(D_MODEL), jax.ShapeDtypeStruct((rs.batch, POOL_HALO, D_MODEL), F32)),
        scratch_shapes=[pltpu.VMEM((POOL_HALO + rs.tm, D_MODEL), F32)],
        compiler_params=_params("arbitrary", "arbitrary"),
        name="pool_ffn",
    )(x, n1, rs.mods, rs.mods, rs.mods, pool_w, pool_scale, n2, rs.mods, rs.mods, rs.mods, *ffn)


def _pool_sample_body(x_ref, st_ref, n1_ref, shift1_ref, scale1_ref, gate1_ref, pw_ref, ps_ref,
                      n2_ref, shift2_ref, scale2_ref, gate2_ref, wg_ref, wu_ref, wd_ref, o_ref, st_out_ref):
    x = x_ref[...]
    h = _prenorm(x, n1_ref[...], shift1_ref[...], scale1_ref[...])
    sums, inv_cnt = [], []
    run = h
    back = 1
    for g, win in enumerate(POOL_WINDOWS):
        lo = g * POOL_GROUP_DIM
        while back < win:
            run = run + st_ref[POOL_STATE - back]
            back += 1
        sums.append(run[:, lo:lo + POOL_GROUP_DIM])
        inv_cnt.append(1.0 / win)
    y = _pool_groups(sums, h, inv_cnt, pw_ref, ps_ref)
    x1 = x + gate1_ref[...] * y
    o_ref[...] = _ffn_tail(x1, n2_ref[...], shift2_ref[...], scale2_ref[...], gate2_ref[...], wg_ref, wu_ref, wd_ref)
    st_out_ref[0:POOL_STATE - 1] = st_ref[1:POOL_STATE]
    st_out_ref[POOL_STATE - 1] = h


def _pool_sample_call(x, state_t, n1, pool_w, pool_scale, n2, mods, ffn):
    bd = x.shape[0]
    g = len(POOL_WINDOWS)
    mod = lambda c: pl.BlockSpec((bd, D_MODEL), lambda i: (0, c), pipeline_mode=pl.Buffered(1))
    return pl.pallas_call(
        _pool_sample_body,
        grid=(1,),
        in_specs=[_resident((bd, D_MODEL)), _resident((POOL_STATE, bd, D_MODEL)), _resident((1, D_MODEL)),
                  mod(0), mod(1), mod(2), _resident((g, POOL_GROUP_DIM, POOL_GROUP_DIM)), _resident((1, D_MODEL)),
                  _resident((1, D_MODEL)), mod(3), mod(4), mod(5)] + _ffn_specs(),
        out_specs=(pl.BlockSpec((bd, D_MODEL), lambda i: (0, 0)),
                   pl.BlockSpec((POOL_STATE, bd, D_MODEL), lambda i: (0, 0, 0))),
        out_shape=(jax.ShapeDtypeStruct((bd, D_MODEL), F32), jax.ShapeDtypeStruct((POOL_STATE, bd, D_MODEL), F32)),
        compiler_params=_params("arbitrary"),
        name="pool_sample",
    )(x, state_t, n1, mods, mods, mods, pool_w, pool_scale, n2, mods, mods, mods, *ffn)


def _gla_gate(hb, wa1_ref, wa2_ref, ba):
    a1 = jnp.dot(hb, wa1_ref[...], preferred_element_type=F32).astype(BF16)
    return _log_sigmoid(jnp.dot(a1, wa2_ref[...], preferred_element_type=F32) + ba) / GLA_TAU


def _gla_proj_body(x_ref, n1_ref, shift_ref, scale_ref, wq_ref, wk_ref, wv_ref, wa1_ref, wa2_ref, ba_ref, wr_ref,
                   q_ref, k_ref, v_ref, g_ref, r_ref):
    hb = _prenorm(x_ref[...], n1_ref[...], shift_ref[...], scale_ref[...]).astype(BF16)
    q_ref[...] = jnp.dot(hb, wq_ref[...], preferred_element_type=F32) * (GLA_DK ** -0.5)
    k_ref[...] = jnp.dot(hb, wk_ref[...], preferred_element_type=F32)
    v_ref[...] = jnp.dot(hb, wv_ref[...], preferred_element_type=F32).astype(v_ref.dtype)
    g_ref[...] = _gla_gate(hb, wa1_ref, wa2_ref, ba_ref[...])
    r_ref[...] = _silu(jnp.dot(hb, wr_ref[...], preferred_element_type=F32))


def _gla_proj_call(x, n1, rs, w):
    kw = GLA_HEADS * GLA_DK
    rank = w["wa1"].shape[1]
    return pl.pallas_call(
        _gla_proj_body,
        grid=rs.grid,
        in_specs=[rs.rows(D_MODEL), _resident((1, D_MODEL)), rs.mod(0), rs.mod(1),
                  _resident((D_MODEL, kw)), _resident((D_MODEL, kw)), _resident((D_MODEL, D_MODEL)),
                  _resident((D_MODEL, rank)), _resident((rank, kw)), _resident((1, kw)), _resident((D_MODEL, D_MODEL))],
        out_specs=(rs.rows(kw), rs.rows(kw), rs.rows(D_MODEL), rs.rows(kw), rs.rows(D_MODEL)),
        out_shape=(rs.shape(kw), rs.shape(kw), rs.shape(D_MODEL, BF16), rs.shape(kw), rs.shape(D_MODEL)),
        compiler_params=_params("arbitrary", "arbitrary"),
        name="gla_proj",
    )(x, n1, rs.mods, rs.mods, w["wq"], w["wk"], w["wv"], w["wa1"], w["wa2"], w["ba"], w["wr"])


def _gla_out_norm(o, gn, r):
    return _rms(o, gn) * r


def _gla_scan_body(q_ref, k_ref, v_ref, g_ref, r_ref, gn_ref, y_ref, s_out_ref, st_ref):
    batch, rows = q_ref.shape[0], q_ref.shape[1]
    c = GLA_CHUNK

    @pl.when(pl.program_id(0) == 0)
    def _():
        st_ref[...] = jnp.zeros_like(st_ref)

    ri = lax.broadcasted_iota(jnp.int32, (c, c), 0)
    ci = lax.broadcasted_iota(jnp.int32, (c, c), 1)
    causal = ci <= ri
    tril = causal.astype(BF16)
    gn = gn_ref[...]

    def chunk(n, carry):
        r0 = pl.multiple_of(n * c, c)
        for b in range(batch):
            q, k, g = (ref[b, pl.ds(r0, c), :] for ref in (q_ref, k_ref, g_ref))
            v, rr = v_ref[b, pl.ds(r0, c), :], r_ref[b, pl.ds(r0, c), :]
            bcum = _dot3_left(tril, g)
            b_last = bcum[c - 1:c, :]
            e = jnp.exp(bcum)
            qe = (q * e).astype(BF16)
            ke = (k * jnp.exp(-bcum)).astype(BF16)
            kd = (k * jnp.exp(b_last - bcum)).astype(BF16)
            decay = jnp.exp(b_last)
            for h in range(GLA_HEADS):
                ks = slice(h * GLA_DK, (h + 1) * GLA_DK)
                vs = slice(h * GLA_DV, (h + 1) * GLA_DV)
                st = st_ref[b, h]
                a = lax.dot_general(qe[:, ks], ke[:, ks], NT_DIMS, preferred_element_type=F32)
                a = jnp.where(causal, a, 0.0).astype(BF16)
                o = (jnp.dot(a, v[:, vs], preferred_element_type=F32)
                     + lax.dot_general(qe[:, ks], st.astype(BF16), NT_DIMS, preferred_element_type=F32))
                st_ref[b, h] = st * decay[:, ks] + lax.dot_general(v[:, vs], kd[:, ks], TN_DIMS,
                                                                  preferred_element_type=F32)
                y_ref[b, pl.ds(r0, c), vs] = _gla_out_norm(o, gn, rr[:, vs]).astype(BF16)
        return carry

    lax.fori_loop(0, rows // c, chunk, 0)

    @pl.when(pl.program_id(0) == pl.num_programs(0) - 1)
    def _():
        for b in range(batch):
            for h in range(GLA_HEADS):
                s_out_ref[b, h] = st_ref[b, h].T


def _gla_scan_call(q, k, v, g, r, gn, rows):
    batch, length, kw = q.shape
    blk = lambda cols: pl.BlockSpec((batch, rows, cols), lambda i: (0, i, 0))
    s_shape = (batch, GLA_HEADS, GLA_DK, GLA_DV)
    return pl.pallas_call(
        _gla_scan_body,
        grid=(length // rows,),
        in_specs=[blk(kw), blk(kw), blk(D_MODEL), blk(kw), blk(D_MODEL), _resident((1, GLA_DV))],
        out_specs=(blk(D_MODEL), pl.BlockSpec(s_shape, lambda i: (0, 0, 0, 0))),
        out_shape=(jax.ShapeDtypeStruct((batch, length, D_MODEL), BF16), jax.ShapeDtypeStruct(s_shape, F32)),
        scratch_shapes=[pltpu.VMEM((batch, GLA_HEADS, GLA_DV, GLA_DK), F32)],
        compiler_params=_params("arbitrary"),
        name="gla_scan",
    )(q, k, v, g, r, gn)


def _gla_sample_proj_body(x_ref, n1_ref, shift_ref, scale_ref, wq_ref, wk_ref, wv_ref, wa1_ref, wa2_ref, ba_ref,
                          wr_ref, q_ref, k_ref, v_ref, g_ref, r_ref):
    hb = _prenorm(x_ref[...], n1_ref[...], shift_ref[...], scale_ref[...]).astype(BF16)
    q_ref[...] = jnp.dot(hb, wq_ref[...], preferred_element_type=F32).T * (GLA_DK ** -0.5)
    k_ref[...] = jnp.dot(hb, wk_ref[...], preferred_element_type=F32).T
    v_ref[...] = jnp.dot(hb, wv_ref[...], preferred_element_type=F32)
    g_ref[...] = _gla_gate(hb, wa1_ref, wa2_ref, ba_ref[...]).T
    r_ref[...] = _silu(jnp.dot(hb, wr_ref[...], preferred_element_type=F32))


def _gla_sample_proj_call(x, n1, mods, w):
    bd = x.shape[0]
    kw = GLA_HEADS * GLA_DK
    rank = w["wa1"].shape[1]
    mod = lambda c: pl.BlockSpec((bd, D_MODEL), lambda i: (0, c), pipeline_mode=pl.Buffered(1))
    full = lambda *s: pl.BlockSpec(s, lambda i: (0,) * len(s))
    return pl.pallas_call(
        _gla_sample_proj_body,
        grid=(1,),
        in_specs=[_resident((bd, D_MODEL)), _resident((1, D_MODEL)), mod(0), mod(1),
                  _resident((D_MODEL, kw)), _resident((D_MODEL, kw)), _resident((D_MODEL, D_MODEL)),
                  _resident((D_MODEL, rank)), _resident((rank, kw)), _resident((1, kw)), _resident((D_MODEL, D_MODEL))],
        out_specs=(full(kw, bd), full(kw, bd), full(bd, D_MODEL), full(kw, bd), full(bd, D_MODEL)),
        out_shape=(jax.ShapeDtypeStruct((kw, bd), F32), jax.ShapeDtypeStruct((kw, bd), F32),
                   jax.ShapeDtypeStruct((bd, D_MODEL), F32), jax.ShapeDtypeStruct((kw, bd), F32),
                   jax.ShapeDtypeStruct((bd, D_MODEL), F32)),
        compiler_params=_params("arbitrary"),
        name="gla_sample_proj",
    )(x, n1, mods, mods, w["wq"], w["wk"], w["wv"], w["wa1"], w["wa2"], w["ba"], w["wr"])


def _gla_decode_body(q_ref, k_ref, g_ref, v_ref, r_ref, gn_ref, s_ref, y_ref, s_out_ref):
    b = pl.program_id(0)
    n = q_ref.shape[1]
    sel = (lax.broadcasted_iota(jnp.int32, (n, n), 0) == b).astype(BF16)
    wide = lambda col: jnp.concatenate([col, col], axis=-1).reshape(GLA_HEADS, GLA_DK, GLA_DV)
    qcol = wide(_lane_column(q_ref[...], sel))
    kcol = wide(_lane_column(k_ref[...], sel))
    decay = wide(jnp.exp(_lane_column(g_ref[...], sel)))
    gn = gn_ref[...]
    for h in range(GLA_HEADS):
        vs = slice(h * GLA_DV, (h + 1) * GLA_DV)
        s_new = s_ref[h] * decay[h] + kcol[h] * v_ref[:, vs]
        s_out_ref[h] = s_new
        o = jnp.sum(qcol[h] * s_new, axis=0, keepdims=True)
        y_ref[:, vs] = _gla_out_norm(o, gn, r_ref[:, vs])


def _gla_decode_call(q_t, k_t, g_t, v, r, gn, state):
    kw, bd = q_t.shape
    assert bd == V7X_LANES, "the sample batch is laid out on the 128 lanes"
    full = lambda *s: pl.BlockSpec(s, lambda i: (0,) * len(s))
    row = pl.BlockSpec((None, 1, D_MODEL), lambda i: (i, 0, 0))
    st = pl.BlockSpec((None, GLA_HEADS, GLA_DK, GLA_DV), lambda i: (i, 0, 0, 0))
    return pl.pallas_call(
        _gla_decode_body,
        grid=(bd,),
        in_specs=[full(kw, bd), full(kw, bd), full(kw, bd), row, row, full(1, GLA_DV), st],
        out_specs=(row, st),
        out_shape=(jax.ShapeDtypeStruct((bd, 1, D_MODEL), F32), jax.ShapeDtypeStruct(state.shape, F32)),
        compiler_params=_params("arbitrary"),
        name="gla_decode",
    )(q_t, k_t, g_t, v, r, gn, state)


def kernel(x_prompt, x_sample, cache_k, cache_v, cache_logf, state_pool, state_gla, page_table, c_prompt, c_sample, norm1, norm2, ada_w, ada_b, fox_wq, fox_wk, fox_wv, fox_wf, fox_bf, fox_qn, fox_kn, fox_wo, pool_w, pool_scale, gla_wq, gla_wk, gla_wv, gla_wa1, gla_wa2, gla_ba, gla_wr, gla_gn, gla_wo, ffn_wg, ffn_wu, ffn_wd):
    batch, length, d = x_prompt.shape
    bd = x_sample.shape[0]
    depth = norm1.shape[0]
    assert d == D_MODEL and x_sample.shape[1] == 1

    c_all = jnp.concatenate([c_prompt, c_sample], axis=0)
    c_all = jnp.pad(c_all, ((0, -c_all.shape[0] % 8), (0, 0)))
    mods = _ada_call(c_all, ada_w, ada_b)

    ck_t = jnp.transpose(cache_k, (0, 1, 3, 4, 2))
    cv_t = jnp.transpose(cache_v, (0, 1, 3, 4, 2))
    clf_t = jnp.transpose(cache_logf, (0, 1, 3, 2))

    xp = x_prompt
    xs = x_sample.reshape(1, bd, d)
    n_fox = (depth + N_MIXERS - 1) // N_MIXERS
    stack_shape = (n_fox, batch, FOX_HEADS, FOX_HEAD_DIM, length)
    kv_stacks = (jnp.zeros(stack_shape, F32), jnp.ones(stack_shape, F32))
    outs = {k: [] for k in ("lf_p", "k_s", "v_s", "lf_s", "pool_p", "pool_s", "gla_p", "gla_s")}
    for i in range(depth):
        kind, j = i % N_MIXERS, i // N_MIXERS
        rp = _Rows(mods[i, :batch], batch, length, ROW_BLOCK)
        rsm = _Rows(mods[i, batch:batch + bd], 1, bd, bd, per_row=True)
        n1, n2 = norm1[i][None], norm2[i][None]
        ffn = (ffn_wg[i].astype(BF16), ffn_wu[i].astype(BF16), ffn_wd[i].astype(BF16))
        if kind == 0:
            w = dict(wq_t=fox_wq[j].T.astype(BF16), wk_t=fox_wk[j].T.astype(BF16), wv_t=fox_wv[j].T.astype(BF16),
                     wf_t=fox_wf[j].T.astype(BF16), bf=fox_bf[j][:, None], qn=fox_qn[j][:, None], kn=fox_kn[j][:, None])
            wo = fox_wo[j].astype(BF16)
            qs_t, ks_t, vs_t, lfs_t = _fox_sample_proj_call(xs, n1, rsm, w)
            qa, kr, *kv_stacks, lf_t, ys = _fox_front_call(xp, n1, mods[i, :batch], w, j, kv_stacks, page_table,
                                                           ck_t, cv_t, clf_t, qs_t[0], ks_t[0], vs_t[0], lfs_t[0])
            yp = _fox_attn_call(qa, kr, kv_stacks[1], j)
            outs["lf_p"].append(jnp.transpose(lf_t, (0, 2, 1)))
            outs["k_s"].append(jnp.transpose(ks_t, (3, 0, 1, 2)))
            outs["v_s"].append(jnp.transpose(vs_t, (3, 0, 1, 2)))
            outs["lf_s"].append(jnp.transpose(lfs_t, (2, 0, 1)))
            xp = _out_ffn_call(xp, yp, wo, n2, rp, ffn)
            xs = _out_ffn_call(xs, ys[None], wo, n2, rsm, ffn)
        elif kind == 1:
            pw, ps = pool_w[j].astype(BF16), pool_scale[j][None]
            xp, tail = _pool_ffn_call(xp, n1, pw, ps, n2, _Rows(mods[i, :batch], batch, length, POOL_ROW_BLOCK), ffn)
            st_t = jnp.transpose(state_pool[j], (1, 0, 2))
            xs2, st_new = _pool_sample_call(xs[0], st_t, n1, pw, ps, n2, rsm.mods, ffn)
            xs = xs2[None]
            outs["pool_p"].append(tail[:, POOL_HALO - POOL_STATE:])
            outs["pool_s"].append(jnp.transpose(st_new, (1, 0, 2)))
        else:
            w = dict(wq=gla_wq[j].astype(BF16), wk=gla_wk[j].astype(BF16), wv=gla_wv[j].astype(BF16),
                     wa1=gla_wa1[j].astype(BF16), wa2=gla_wa2[j].astype(BF16), ba=gla_ba[j][None],
                     wr=gla_wr[j].astype(BF16))
            wo, gn = gla_wo[j].astype(BF16), gla_gn[j][None]
            q, k, v, g, r = _gla_proj_call(xp, n1, rp, w)
            yp, s_fin = _gla_scan_call(q, k, v, g, r, gn, 4 * GLA_CHUNK)
            q_t, k_t, vs_, g_t, r_s = _gla_sample_proj_call(xs[0], n1, rsm.mods, w)
            ys, s_new = _gla_decode_call(q_t, k_t, g_t, vs_[:, None], r_s[:, None], gn, state_gla[j])
            outs["gla_p"].append(s_fin)
            outs["gla_s"].append(s_new)
            xp = _out_ffn_call(xp, yp, wo, n2, rp, ffn)
            xs = _out_ffn_call(xs, ys.reshape(1, bd, d).astype(BF16), wo, n2, rsm, ffn)

    stack = lambda name: jnp.stack(outs[name])
    return (xp, xs.reshape(bd, 1, d),
            jnp.transpose(kv_stacks[0], (0, 1, 4, 2, 3)), jnp.transpose(kv_stacks[1], (0, 1, 4, 2, 3)), stack("lf_p"),
            stack("k_s"), stack("v_s"), stack("lf_s"),
            stack("pool_p"), stack("pool_s"), stack("gla_p"), stack("gla_s"))
```

```python
import functools
import math

import jax
import jax.numpy as jnp
import numpy as np
from jax import lax
from jax.experimental import pallas as pl
from jax.experimental.pallas import tpu as pltpu

F32, BF16 = jnp.float32, jnp.bfloat16

D_MODEL = 1024
N_MIXERS = 3
PAGE_SIZE = 128
FOX_HEADS = 16
FOX_HEAD_DIM = 64
FOX_SCALE = FOX_HEAD_DIM ** -0.5
LOG2E = math.log2(math.e)
FOX_AUG_ROWS = 16
POOL_WINDOWS = (2, 4, 8, 16)
POOL_GROUP_DIM = D_MODEL // len(POOL_WINDOWS)
POOL_STATE = max(POOL_WINDOWS) - 1
POOL_HALO = 16
GLA_HEADS = 4
GLA_DK = 128
GLA_DV = 256
GLA_TAU = 16.0
GLA_CHUNK = 64
D_FF = 2816
EPS = 1e-6

V7X_LANES = 128
V7X_VMEM_LIMIT_BYTES = 56 * 1024 * 1024

ROW_BLOCK = 512
POOL_ROW_BLOCK = 256
ATTN_BLOCK = 512
ATTN_HEADS = 8
V7X_MXU_DIM = 256
FF_CHUNK = 6 * V7X_MXU_DIM
ADA_COLS = 1536

NT_DIMS = (((1,), (1,)), ((), ()))
TN_DIMS = (((0,), (0,)), ((), ()))


def _params(*sem):
    return pltpu.CompilerParams(dimension_semantics=sem, vmem_limit_bytes=V7X_VMEM_LIMIT_BYTES)


def _resident(shape):
    zeros = (0,) * len(shape)
    return pl.BlockSpec(shape, lambda *_: zeros, pipeline_mode=pl.Buffered(1))


def _silu(x):
    return x * jax.nn.sigmoid(x)


def _log_sigmoid(x):
    return jnp.minimum(x, 0.0) - jnp.log1p(jnp.exp(-jnp.abs(x)))


def _rms(x, g):
    return x * lax.rsqrt(jnp.mean(x * x, axis=-1, keepdims=True) + EPS) * g


def _prenorm(x, g, shift, scale):
    return _rms(x, g) * (1.0 + scale) + shift


def _split3(x):
    hi = x.astype(BF16)
    r1 = x - hi.astype(F32)
    mid = r1.astype(BF16)
    lo = (r1 - mid.astype(F32)).astype(BF16)
    return hi, mid, lo


def _dot3(x, m01):
    hi, mid, lo = _split3(x)
    return (jnp.dot(hi, m01, preferred_element_type=F32)
            + jnp.dot(mid, m01, preferred_element_type=F32)
            + jnp.dot(lo, m01, preferred_element_type=F32))


def _dot3_left(m01, x):
    hi, mid, lo = _split3(x)
    return (jnp.dot(m01, hi, preferred_element_type=F32)
            + jnp.dot(m01, mid, preferred_element_type=F32)
            + jnp.dot(m01, lo, preferred_element_type=F32))


def _ffn_tail(x1, n2, shift2, scale2, gate2, wg_ref, wu_ref, wd_ref):
    f = _prenorm(x1, n2, shift2, scale2).astype(BF16)
    acc = jnp.zeros(x1.shape, F32)
    for c0 in range(0, D_FF, FF_CHUNK):
        c1 = min(c0 + FF_CHUNK, D_FF)
        g = jnp.dot(f, wg_ref[:, c0:c1], preferred_element_type=F32)
        u = jnp.dot(f, wu_ref[:, c0:c1], preferred_element_type=F32)
        a = (_silu(g) * u).astype(BF16)
        acc = acc + jnp.dot(a, wd_ref[c0:c1, :], preferred_element_type=F32)
    return x1 + gate2 * acc


def _ada_body(c_ref, w_ref, b_ref, o_ref):
    a = _silu(c_ref[...]).astype(BF16)
    o_ref[...] = jnp.dot(a, w_ref[...].astype(BF16), preferred_element_type=F32) + b_ref[...]


def _ada_call(c_all, ada_w, ada_b):
    depth, d, n = ada_w.shape
    rows = c_all.shape[0]
    return pl.pallas_call(
        _ada_body,
        grid=(depth, n // ADA_COLS),
        in_specs=[
            pl.BlockSpec((rows, d), lambda i, j: (0, 0)),
            pl.BlockSpec((None, d, ADA_COLS), lambda i, j: (i, 0, j)),
            pl.BlockSpec((None, 1, ADA_COLS), lambda i, j: (i, 0, j)),
        ],
        out_specs=pl.BlockSpec((None, rows, ADA_COLS), lambda i, j: (i, 0, j)),
        out_shape=jax.ShapeDtypeStruct((depth, rows, n), F32),
        compiler_params=_params("arbitrary", "arbitrary"),
        name="ada_mods",
    )(c_all, ada_w, ada_b.reshape(depth, 1, n))


class _Rows:
    def __init__(self, mods, batch, length, tm, per_row=False):
        self.batch, self.length, self.tm = batch, length, tm
        self.grid = (batch, length // tm)
        self.per_row = per_row
        self.mods = mods if per_row else mods.reshape(batch, 6, 1, D_MODEL)

    def rows(self, cols):
        return pl.BlockSpec((None, self.tm, cols), lambda b, i: (b, i, 0))

    def mod(self, c):
        if self.per_row:
            return pl.BlockSpec((self.tm, D_MODEL), lambda b, i: (i, c))
        return pl.BlockSpec((None, None, 1, D_MODEL), lambda b, i: (b, c, 0, 0))

    def shape(self, cols, dtype=F32):
        return jax.ShapeDtypeStruct((self.batch, self.length, cols), dtype)


def _head_norm(y, gain_col):
    ms = jnp.mean(y * y, axis=1, keepdims=True)
    return y * lax.rsqrt(ms + EPS) * gain_col[None, :, :]


FOX_PROJ_INPUTS = 12


def _fox_proj_body(*refs, with_bias_rows):
    _fox_proj_compute(pl.program_id(1) == 0, *refs, with_bias_rows=with_bias_rows)


def _fox_proj_compute(first_block, x_ref, n1_ref, shift_ref, scale_ref, wq_ref, wk_ref, wv_ref, wf_ref, bf_ref,
                      qn_ref, kn_ref, tri_ref, *out_refs, with_bias_rows):
    tm = x_ref.shape[0]
    hb = _prenorm(x_ref[...], n1_ref[...], shift_ref[...], scale_ref[...]).astype(BF16)

    def proj_t(w_ref):
        return lax.dot_general(w_ref[...], hb, NT_DIMS, preferred_element_type=F32)

    q = _head_norm(proj_t(wq_ref).reshape(FOX_HEADS, FOX_HEAD_DIM, tm), qn_ref[...])
    k = _head_norm(proj_t(wk_ref).reshape(FOX_HEADS, FOX_HEAD_DIM, tm), kn_ref[...])
    v = proj_t(wv_ref).reshape(FOX_HEADS, FOX_HEAD_DIM, tm)
    lf = _log_sigmoid(proj_t(wf_ref) + bf_ref[...])

    if not with_bias_rows:
        q_ref, k_ref, v_ref, lf_ref = out_refs
        q_ref[...], k_ref[...], v_ref[...], lf_ref[...] = q, k, v, lf
        return

    qa_ref, kr_ref, k_ref, v_ref, lf_ref, carry_ref = out_refs
    k_ref[...], v_ref[...], lf_ref[...] = k, v, lf

    @pl.when(first_block)
    def _():
        carry_ref[...] = jnp.zeros_like(carry_ref)

    f_cum = _dot3(lf, tri_ref[...]) + carry_ref[:, 0:1]
    carry_ref[...] = jnp.broadcast_to(f_cum[:, tm - 1:tm], carry_ref.shape)

    hi, mid, lo = (p.astype(F32)[:, None, :] for p in _split3(f_cum * LOG2E))
    r = lax.broadcasted_iota(jnp.int32, (FOX_HEADS, FOX_AUG_ROWS, tm), 1)
    q_rows = jnp.where(r == 0, hi, jnp.where(r == 1, mid, jnp.where(r == 2, lo, jnp.where(r < 6, 1.0, 0.0))))
    k_rows = jnp.where(r < 3, 1.0, jnp.where(r == 3, -hi, jnp.where(r == 4, -mid, jnp.where(r == 5, -lo, 0.0))))
    pad =jnp.zeros((FOX_HEADS, 2 * FOX_HEAD_DIM - FOX_HEAD_DIM - FOX_AUG_ROWS, tm), BF16)
    qa_ref[:, 0:FOX_HEAD_DIM, :] = (q * (FOX_SCALE * LOG2E)).astype(BF16)
    qa_ref[:, FOX_HEAD_DIM:FOX_HEAD_DIM + FOX_AUG_ROWS, :] = q_rows.astype(BF16)
    qa_ref[:, FOX_HEAD_DIM + FOX_AUG_ROWS:, :] = pad
    ka = jnp.concatenate([k.astype(BF16), k_rows.astype(BF16), pad], axis=1)
    for h in range(FOX_HEADS):
        kr_ref[h] = ka[h].T


def _fox_prompt_proj_body(*refs):
    refs = refs[:FOX_PROJ_INPUTS] + refs[FOX_PROJ_INPUTS + 2:]
    _fox_proj_compute(pl.program_id(1) == 0, *refs, with_bias_rows=True)


def _fox_prompt_proj_call(x, n1, rs, w, layer, stacks):
    batch, length, tm = rs.batch, rs.length, rs.tm
    pair_rows = 2 * FOX_HEAD_DIM
    st_shape = jax.ShapeDtypeStruct(stacks[0].shape, F32)
    st_spec = pl.BlockSpec((None, None, FOX_HEADS, FOX_HEAD_DIM, tm), lambda b, i: (layer, b, 0, 0, i))
    tri = jnp.triu(jnp.ones((tm, tm), BF16))
    operands = (x, n1, rs.mods, rs.mods, w["wq_t"], w["wk_t"], w["wv_t"], w["wf_t"], w["bf"], w["qn"], w["kn"], tri)
    assert len(operands) == FOX_PROJ_INPUTS
    return pl.pallas_call(
        _fox_prompt_proj_body,
        grid=rs.grid,
        in_specs=[rs.rows(D_MODEL), _resident((1, D_MODEL)), rs.mod(0), rs.mod(1),
                  _resident((D_MODEL, D_MODEL)), _resident((D_MODEL, D_MODEL)), _resident((D_MODEL, D_MODEL)),
                  _resident((FOX_HEADS, D_MODEL)), _resident((FOX_HEADS, 1)),
                  _resident((FOX_HEAD_DIM, 1)), _resident((FOX_HEAD_DIM, 1)), _resident((tm, tm)),
                  pl.BlockSpec(memory_space=pl.ANY), pl.BlockSpec(memory_space=pl.ANY)],
        out_specs=(pl.BlockSpec((None, FOX_HEADS, pair_rows, tm), lambda b, i: (b, 0, 0, i)),
                   pl.BlockSpec((None, FOX_HEADS, tm, pair_rows), lambda b, i: (b, 0, i, 0)),
                   st_spec, st_spec,
                   pl.BlockSpec((None, FOX_HEADS, tm), lambda b, i: (b, 0, i))),
        out_shape=(jax.ShapeDtypeStruct((batch, FOX_HEADS, pair_rows, length), BF16),
                   jax.ShapeDtypeStruct((batch, FOX_HEADS, length, pair_rows), BF16),
                   st_shape, st_shape,
                   jax.ShapeDtypeStruct((batch, FOX_HEADS, length), F32)),
        scratch_shapes=[pltpu.VMEM((FOX_HEADS, V7X_LANES), F32)],
        input_output_aliases={FOX_PROJ_INPUTS: 2, FOX_PROJ_INPUTS + 1: 3},
        compiler_params=_params("arbitrary", "arbitrary"),
        name="fox_proj_bias",
    )(*operands, *stacks)


def _fox_sample_proj_call(x, n1, rs, w):
    batch, length, tm = rs.batch, rs.length, rs.tm
    hshape = jax.ShapeDtypeStruct((batch, FOX_HEADS, FOX_HEAD_DIM, length), F32)
    hspec = pl.BlockSpec((None, FOX_HEADS, FOX_HEAD_DIM, tm), lambda b, i: (b, 0, 0, i))
    tri = jnp.triu(jnp.ones((tm, tm), BF16))
    return pl.pallas_call(
        functools.partial(_fox_proj_body, with_bias_rows=False),
        grid=rs.grid,
        in_specs=[rs.rows(D_MODEL), _resident((1, D_MODEL)), rs.mod(0), rs.mod(1),
                  _resident((D_MODEL, D_MODEL)), _resident((D_MODEL, D_MODEL)), _resident((D_MODEL, D_MODEL)),
                  _resident((FOX_HEADS, D_MODEL)), _resident((FOX_HEADS, 1)),
                  _resident((FOX_HEAD_DIM, 1)), _resident((FOX_HEAD_DIM, 1)), _resident((tm, tm))],
        out_specs=(hspec,) * 3 + (pl.BlockSpec((None, FOX_HEADS, tm), lambda b, i: (b, 0, i)),),
        out_shape=(hshape,) * 3 + (jax.ShapeDtypeStruct((batch, FOX_HEADS, length), F32),),
        compiler_params=_params("arbitrary", "arbitrary"),
        name="fox_proj",
    )(x, n1, rs.mods, rs.mods, w["wq_t"], w["wk_t"], w["wv_t"], w["wf_t"], w["bf"], w["qn"], w["kn"], tri)


def _fox_attn_body(qi_tab, ki_tab, q_ref, k_ref, v_ref, o_ref, m_ref, acc_ref):
    t = pl.program_id(2)
    qi, ki = qi_tab[t], ki_tab[t]
    nh, tk, tq = k_ref.shape[0], k_ref.shape[1], q_ref.shape[2]
    hd = FOX_HEAD_DIM

    @pl.when(ki == 0)
    def _():
        m_ref[...] = jnp.full_like(m_ref, -jnp.inf)
        acc_ref[...] = jnp.zeros_like(acc_ref)

    def step(diagonal):
        scores = [jnp.dot(k_ref[hh], q_ref[hh], preferred_element_type=F32) for hh in range(nh)]
        ones_row = (lax.broadcasted_iota(jnp.int32, (hd, tk), 0) == 0).astype(BF16)
        for hh in range(nh):
            s = scores[hh]
            if diagonal:
                key = lax.broadcasted_iota(jnp.int32, (tk, tq), 0)
                qry = lax.broadcasted_iota(jnp.int32, (tk, tq), 1)
                s = jnp.where(key > qry, -jnp.inf, s)
            m_prev = m_ref[hh]
            m_new = jnp.maximum(m_prev, jnp.max(s, axis=0, keepdims=True))
            p = jnp.exp2(s - m_new).astype(BF16)
            m_ref[hh] = m_new
            vb = v_ref[hh * hd:(hh + 1) * hd, :].astype(BF16)
            v_aug = jnp.concatenate([vb, ones_row] if hh % 2 == 0 else [ones_row, vb], axis=0)
            acc_ref[hh] = (acc_ref[hh] * jnp.exp2(m_prev - m_new)
                           + jnp.dot(v_aug, p, preferred_element_type=F32))

    @pl.when(ki < qi)
    def _():
        step(False)

    @pl.when(ki == qi)
    def _():
        step(True)
        first = lax.broadcasted_iota(jnp.int32, (2 * hd, tq), 0) < hd
        for pair in range(nh // 2):
            a0, a1 = acc_ref[2 * pair], acc_ref[2 * pair + 1]
            o_t = jnp.where(first, a0 / a0[hd:hd + 1, :], a1 / a1[0:1, :])
            o_ref[:, pair * 2 * hd:(pair + 1) * 2 * hd] = o_t.T.astype(o_ref.dtype)


def _fox_attn_call(qa, kr, v_stack, layer):
    batch, _, rows, length = qa.shape
    tq = tk = ATTN_BLOCK
    nh = ATTN_HEADS
    groups = FOX_HEADS // nh
    nq = length // tq
    pairs = [(i, j) for i in range(nq) for j in range(i + 1)]
    qi_tab = jnp.asarray([p[0] for p in pairs], jnp.int32)
    ki_tab = jnp.asarray([p[1] for p in pairs], jnp.int32)
    qa = qa.reshape(batch, groups, nh, rows, length)
    kr = kr.reshape(batch, groups, nh, length, rows)
    v_stack = v_stack.reshape(v_stack.shape[0], batch, groups, nh * FOX_HEAD_DIM, length)
    return pl.pallas_call(
        _fox_attn_body,
        grid_spec=pltpu.PrefetchScalarGridSpec(
            num_scalar_prefetch=2,
            grid=(batch, groups, len(pairs)),
            in_specs=[
                pl.BlockSpec((None, None, nh, rows, tq), lambda b, h, t, qt, kt: (b, h, 0, 0, qt[t])),
                pl.BlockSpec((None, None, nh, tk, rows), lambda b, h, t, qt, kt: (b, h, 0, kt[t], 0)),
                pl.BlockSpec((None, None, None, nh * FOX_HEAD_DIM, tk), lambda b, h, t, qt, kt: (layer, b, h, 0, kt[t])),
            ],
            out_specs=pl.BlockSpec((None, tq, nh * FOX_HEAD_DIM), lambda b, h, t, qt, kt: (b, qt[t], h)),
            scratch_shapes=[pltpu.VMEM((nh, 1, tq), F32), pltpu.VMEM((nh, rows, tq), F32)],
        ),
        out_shape=jax.ShapeDtypeStruct((batch, length, D_MODEL), BF16),
        compiler_params=_params("arbitrary", "arbitrary", "arbitrary"),
        name="fox_attn",
    )(qi_tab, ki_tab, qa, kr, v_stack)


def _lane_column(x_t, sel):
    return _dot3(x_t, sel)


def _fox_decode_compute(*refs):
    n_pages = (len(refs) - 6) // 3
    ck_refs, cv_refs, clf_refs = refs[:n_pages], refs[n_pages:2 * n_pages], refs[2 * n_pages:3 * n_pages]
    q_ref, k_ref, v_ref, lf_ref, o_ref, ot_ref = refs[3 * n_pages:]
    b = pl.program_id(0)
    hs = (FOX_HEADS, FOX_HEAD_DIM, PAGE_SIZE)
    rows = FOX_HEADS * FOX_HEAD_DIM
    lane = lax.broadcasted_iota(jnp.int32, (FOX_HEADS, PAGE_SIZE), 1)

    @pl.when(b == 0)
    def _():
        ot_ref[...] = jnp.zeros_like(ot_ref)

    sel = (lax.broadcasted_iota(jnp.int32, (PAGE_SIZE, PAGE_SIZE), 0) == b).astype(BF16)
    qcol = _lane_column(q_ref[...].reshape(rows, PAGE_SIZE), sel).reshape(hs) * FOX_SCALE
    kcol = _lane_column(k_ref[...].reshape(rows, PAGE_SIZE), sel).reshape(hs)
    vcol = _lane_column(v_ref[...].reshape(rows, PAGE_SIZE), sel).reshape(hs)
    s_self = jnp.sum(qcol * kcol, axis=1)

    lf_all = jnp.concatenate([r[...] for r in clf_refs], axis=0)
    after = (lax.broadcasted_iota(jnp.int32, (PAGE_SIZE, PAGE_SIZE), 0)
             > lax.broadcasted_iota(jnp.int32, (PAGE_SIZE, PAGE_SIZE), 1)).astype(BF16)
    suffix_all = _dot3(lf_all, after)
    later = _lane_column(lf_ref[...], sel)
    scores = [None] * n_pages
    for p in reversed(range(n_pages)):
        pg = slice(p * FOX_HEADS, (p + 1) * FOX_HEADS)
        scores[p] = jnp.sum(qcol * ck_refs[p][...], axis=1) + later + suffix_all[pg]
        later = later + jnp.broadcast_to(suffix_all[pg, 0:1] + lf_all[pg, 0:1], lane.shape)

    m = s_self
    for s in scores:
        m = jnp.maximum(m, s)
    m = jnp.max(m, axis=-1, keepdims=True)
    p_self = jnp.exp(s_self - m)
    den = jnp.where(lane == 0, p_self, 0.0)
    acc = jnp.where(lane[:, None, :] == 0, vcol * p_self[:, None, :], 0.0)
    for p in range(n_pages):
        pr = jnp.exp(scores[p] - m)
        den = den + pr
        acc = acc + cv_refs[p][...] * pr[:, None, :]
    num = jnp.sum(acc, axis=-1, keepdims=True)
    den = jnp.sum(den, axis=-1, keepdims=True)[:, None, :]
    o = jnp.broadcast_to(num / den, hs)
    ot_ref[...] = jnp.where(lane[:, None, :] == b, o, ot_ref[...])

    @pl.when(b == pl.num_programs(0) - 1)
    def _():
        o_ref[...] = ot_ref[...].reshape(rows, PAGE_SIZE).T.astype(o_ref.dtype)


def _fox_decode_body(pt_ref, *refs):
    _fox_decode_compute(*refs)


def _fox_decode_call(page_table, ck_t, cv_t, clf_t, j, q_t, k_t, v_t, lf_t):
    n_seq, n_pages = page_table.shape
    assert n_seq == PAGE_SIZE, "the sample batch is laid out on the 128 lanes"
    hs = (FOX_HEADS, FOX_HEAD_DIM, PAGE_SIZE)
    kv_page = lambda p: pl.BlockSpec((None, None) + hs, lambda b, pt: (j, pt[b, p], 0, 0, 0))
    lf_page = lambda p: pl.BlockSpec((None, None, FOX_HEADS, PAGE_SIZE), lambda b, pt: (j, pt[b, p], 0, 0))
    whole = lambda shape: pl.BlockSpec(shape, lambda b, pt: (0,) * len(shape))
    pages = range(n_pages)
    return pl.pallas_call(
        _fox_decode_body,
        grid_spec=pltpu.PrefetchScalarGridSpec(
            num_scalar_prefetch=1,
            grid=(n_seq,),
            in_specs=([kv_page(p) for p in pages] + [kv_page(p) for p in pages] + [lf_page(p) for p in pages]
                      + [whole(hs), whole(hs), whole(hs), whole((FOX_HEADS, PAGE_SIZE))]),
            out_specs=whole((n_seq, D_MODEL)),
            scratch_shapes=[pltpu.VMEM(hs, F32)],
        ),
        out_shape=jax.ShapeDtypeStruct((n_seq, D_MODEL), BF16),
        compiler_params=_params("arbitrary"),
        name="fox_decode",
    )(page_table, *([ck_t] * n_pages), *([cv_t] * n_pages), *([clf_t] * n_pages), q_t, k_t, v_t, lf_t)


def _out_ffn_body(x_ref, y_ref, wo_ref, gate1_ref, n2_ref, shift2_ref, scale2_ref, gate2_ref,
                  wg_ref, wu_ref, wd_ref, o_ref):
    mix = jnp.dot(y_ref[...], wo_ref[...], preferred_element_type=F32)
    x1 = x_ref[...] + gate1_ref[...] * mix
    o_ref[...] = _ffn_tail(x1, n2_ref[...], shift2_ref[...], scale2_ref[...], gate2_ref[...], wg_ref, wu_ref, wd_ref)


def _ffn_specs():
    return [_resident((D_MODEL, D_FF)), _resident((D_MODEL, D_FF)), _resident((D_FF, D_MODEL))]


def _out_ffn_call(x, y, wo, n2, rs, ffn):
    return pl.pallas_call(
        _out_ffn_body,
        grid=rs.grid,
        in_specs=[rs.rows(D_MODEL), rs.rows(D_MODEL), _resident((D_MODEL, D_MODEL)), rs.mod(2),
                  _resident((1, D_MODEL)), rs.mod(3), rs.mod(4), rs.mod(5)] + _ffn_specs(),
        out_specs=rs.rows(D_MODEL),
        out_shape=rs.shape(D_MODEL),
        compiler_params=_params("arbitrary", "arbitrary"),
        name="out_ffn",
    )(x, y, wo, rs.mods, n2, rs.mods, rs.mods, rs.mods, *ffn)


def _pool_groups(window_sums, h, inv_cnt, pw_ref, ps_ref):
    outs = []
    for g in range(len(POOL_WINDOWS)):
        sl = slice(g * POOL_GROUP_DIM, (g + 1) * POOL_GROUP_DIM)
        d = (window_sums[g] * inv_cnt[g] - h[:, sl]).astype(BF16)
        outs.append(jnp.dot(d, pw_ref[g], preferred_element_type=F32))
    return jnp.concatenate(outs, axis=-1) * ps_ref[...]


def _pool_ffn_body(x_ref, n1_ref, shift1_ref, scale1_ref, gate1_ref, pw_ref, ps_ref,
                   n2_ref, shift2_ref, scale2_ref, gate2_ref, wg_ref, wu_ref, wd_ref,
                   o_ref, tail_ref, ext_ref):
    tm = x_ref.shape[0]
    i = pl.program_id(1)
    x = x_ref[...]
    h = _prenorm(x, n1_ref[...], shift1_ref[...], scale1_ref[...])

    @pl.when(i == 0)
    def _():
        ext_ref[0:POOL_HALO, :] = jnp.zeros((POOL_HALO, D_MODEL), F32)

    ext_ref[POOL_HALO:, :] = h
    pos = i * tm + lax.broadcasted_iota(jnp.int32, (tm, 1), 0)
    sums, inv_cnt = [], []
    run, back = h, 1
    for g, win in enumerate(POOL_WINDOWS):
        lo = g * POOL_GROUP_DIM
        while back < win:
            run = run + ext_ref[POOL_HALO - back:POOL_HALO - back + tm, lo:]
            back += 1
        sums.append(run[:, :POOL_GROUP_DIM])
        if g + 1 < len(POOL_WINDOWS):
            run = run[:, POOL_GROUP_DIM:]
        inv_cnt.append(1.0 / jnp.minimum(win, pos + 1).astype(F32))
    y = _pool_groups(sums, h, inv_cnt, pw_ref, ps_ref)
    x1 = x + gate1_ref[...] * y
    o_ref[...] = _ffn_tail(x1, n2_ref[...], shift2_ref[...], scale2_ref[...], gate2_ref[...], wg_ref, wu_ref, wd_ref)
    halo = ext_ref[tm:tm + POOL_HALO, :]
    ext_ref[0:POOL_HALO, :] = halo
    tail_ref[...] = halo


def _pool_ffn_call(x, n1, pool_w, pool_scale, n2, rs, ffn):
    g = len(POOL_WINDOWS)
    return pl.pallas_call(
        _pool_ffn_body,
        grid=rs.grid,
        in_specs=[rs.rows(D_MODEL), _resident((1, D_MODEL)), rs.mod(0), rs.mod(1), rs.mod(2),
                  _resident((g, POOL_GROUP_DIM, POOL_GROUP_DIM)), _resident((1, D_MODEL)),
                  _resident((1, D_MODEL)), rs.mod(3), rs.mod(4), rs.mod(5)] + _ffn_specs(),
        out_specs=(rs.rows(D_MODEL), pl.BlockSpec((None, POOL_HALO, D_MODEL), lambda b, i: (b, 0, 0))),
        out_shape=(rs.shape(D_MODEL), jax.ShapeDtypeStruct((rs.batch, POOL_HALO, D_MODEL), F32)),
        scratch_shapes=[pltpu.VMEM((POOL_HALO + rs.tm, D_MODEL), F32)],
        compiler_params=_params("arbitrary", "arbitrary"),
        name="pool_ffn",
    )(x, n1, rs.mods, rs.mods, rs.mods, pool_w, pool_scale, n2, rs.mods, rs.mods, rs.mods, *ffn)


def _pool_sample_body(x_ref, st_ref, n1_ref, shift1_ref, scale1_ref, gate1_ref, pw_ref, ps_ref,
                      n2_ref, shift2_ref, scale2_ref, gate2_ref, wg_ref, wu_ref, wd_ref, o_ref, st_out_ref):
    x = x_ref[...]
    h = _prenorm(x, n1_ref[...], shift1_ref[...], scale1_ref[...])
    sums, inv_cnt = [], []
    run = h
    back = 1
    for g, win in enumerate(POOL_WINDOWS):
        lo = g * POOL_GROUP_DIM
        while back < win:
            run = run + st_ref[POOL_STATE - back]
            back += 1
        sums.append(run[:, lo:lo + POOL_GROUP_DIM])
        inv_cnt.append(1.0 / win)
    y = _pool_groups(sums, h, inv_cnt, pw_ref, ps_ref)
    x1 = x + gate1_ref[...] * y
    o_ref[...] = _ffn_tail(x1, n2_ref[...], shift2_ref[...], scale2_ref[...], gate2_ref[...], wg_ref, wu_ref, wd_ref)
    st_out_ref[0:POOL_STATE - 1] = st_ref[1:POOL_STATE]
    st_out_ref[POOL_STATE - 1] = h


def _pool_sample_call(x, state_t, n1, pool_w, pool_scale, n2, mods, ffn):
    bd = x.shape[0]
    g = len(POOL_WINDOWS)
    mod = lambda c: pl.BlockSpec((bd, D_MODEL), lambda i: (0, c), pipeline_mode=pl.Buffered(1))
    return pl.pallas_call(
        _pool_sample_body,
        grid=(1,),
        in_specs=[_resident((bd, D_MODEL)), _resident((POOL_STATE, bd, D_MODEL)), _resident((1, D_MODEL)),
                  mod(0), mod(1), mod(2), _resident((g, POOL_GROUP_DIM, POOL_GROUP_DIM)), _resident((1, D_MODEL)),
                  _resident((1, D_MODEL)), mod(3), mod(4), mod(5)] + _ffn_specs(),
        out_specs=(pl.BlockSpec((bd, D_MODEL), lambda i: (0, 0)),
                   pl.BlockSpec((POOL_STATE, bd, D_MODEL), lambda i: (0, 0, 0))),
        out_shape=(jax.ShapeDtypeStruct((bd, D_MODEL), F32), jax.ShapeDtypeStruct((POOL_STATE, bd, D_MODEL), F32)),
        compiler_params=_params("arbitrary"),
        name="pool_sample",
    )(x, state_t, n1, mods, mods, mods, pool_w, pool_scale, n2, mods, mods, mods, *ffn)


def _gla_gate(hb, wa1_ref, wa2_ref, ba):
    a1 = jnp.dot(hb, wa1_ref[...], preferred_element_type=F32).astype(BF16)
    return _log_sigmoid(jnp.dot(a1, wa2_ref[...], preferred_element_type=F32) + ba) / GLA_TAU


def _gla_proj_body(x_ref, n1_ref, shift_ref, scale_ref, wq_ref, wk_ref, wv_ref, wa1_ref, wa2_ref, ba_ref, wr_ref,
                   q_ref, k_ref, v_ref, g_ref, r_ref):
    hb = _prenorm(x_ref[...], n1_ref[...], shift_ref[...], scale_ref[...]).astype(BF16)
    q_ref[...] = jnp.dot(hb, wq_ref[...], preferred_element_type=F32) * (GLA_DK ** -0.5)
    k_ref[...] = jnp.dot(hb, wk_ref[...], preferred_element_type=F32)
    v_ref[...] = jnp.dot(hb, wv_ref[...], preferred_element_type=F32).astype(v_ref.dtype)
    g_ref[...] = _gla_gate(hb, wa1_ref, wa2_ref, ba_ref[...])
    r_ref[...] = _silu(jnp.dot(hb, wr_ref[...], preferred_element_type=F32))


def _gla_proj_call(x, n1, rs, w):
    kw = GLA_HEADS * GLA_DK
    rank = w["wa1"].shape[1]
    return pl.pallas_call(
        _gla_proj_body,
        grid=rs.grid,
        in_specs=[rs.rows(D_MODEL), _resident((1, D_MODEL)), rs.mod(0), rs.mod(1),
                  _resident((D_MODEL, kw)), _resident((D_MODEL, kw)), _resident((D_MODEL, D_MODEL)),
                  _resident((D_MODEL, rank)), _resident((rank, kw)), _resident((1, kw)), _resident((D_MODEL, D_MODEL))],
        out_specs=(rs.rows(kw), rs.rows(kw), rs.rows(D_MODEL), rs.rows(kw), rs.rows(D_MODEL)),
        out_shape=(rs.shape(kw), rs.shape(kw), rs.shape(D_MODEL, BF16), rs.shape(kw), rs.shape(D_MODEL)),
        compiler_params=_params("arbitrary", "arbitrary"),
        name="gla_proj",
    )(x, n1, rs.mods, rs.mods, w["wq"], w["wk"], w["wv"], w["wa1"], w["wa2"], w["ba"], w["wr"])


def _gla_out_norm(o, gn, r):
    return _rms(o, gn) * r


def _gla_scan_body(q_ref, k_ref, v_ref, g_ref, r_ref, gn_ref, y_ref, s_out_ref, st_ref):
    batch, rows = q_ref.shape[0], q_ref.shape[1]
    c = GLA_CHUNK

    @pl.when(pl.program_id(0) == 0)
    def _():
        st_ref[...] = jnp.zeros_like(st_ref)

    ri = lax.broadcasted_iota(jnp.int32, (c, c), 0)
    ci = lax.broadcasted_iota(jnp.int32, (c, c), 1)
    causal = ci <= ri
    tril = causal.astype(BF16)
    gn = gn_ref[...]

    def chunk(n, carry):
        r0 = pl.multiple_of(n * c, c)
        for b in range(batch):
            q, k, g = (ref[b, pl.ds(r0, c), :] for ref in (q_ref, k_ref, g_ref))
            v, rr = v_ref[b, pl.ds(r0, c), :], r_ref[b, pl.ds(r0, c), :]
            bcum = _dot3_left(tril, g)
            b_last = bcum[c - 1:c, :]
            e = jnp.exp(bcum)
            qe = (q * e).astype(BF16)
            ke = (k * jnp.exp(-bcum)).astype(BF16)
            kd = (k * jnp.exp(b_last - bcum)).astype(BF16)
            decay = jnp.exp(b_last)
            for h in range(GLA_HEADS):
                ks = slice(h * GLA_DK, (h + 1) * GLA_DK)
                vs = slice(h * GLA_DV, (h + 1) * GLA_DV)
                st = st_ref[b, h]
                a = lax.dot_general(qe[:, ks], ke[:, ks], NT_DIMS, preferred_element_type=F32)
                a = jnp.where(causal, a, 0.0).astype(BF16)
                o = (jnp.dot(a, v[:, vs], preferred_element_type=F32)
                     + lax.dot_general(qe[:, ks], st.astype(BF16), NT_DIMS, preferred_element_type=F32))
                st_ref[b, h] = st * decay[:, ks] + lax.dot_general(v[:, vs], kd[:, ks], TN_DIMS,
                                                                  preferred_element_type=F32)
                y_ref[b, pl.ds(r0, c), vs] = _gla_out_norm(o, gn, rr[:, vs]).astype(BF16)
        return carry

    lax.fori_loop(0, rows // c, chunk, 0)

    @pl.when(pl.program_id(0) == pl.num_programs(0) - 1)
    def _():
        for b in range(batch):
            for h in range(GLA_HEADS):
                s_out_ref[b, h] = st_ref[b, h].T


def _gla_scan_call(q, k, v, g, r, gn, rows):
    batch, length, kw = q.shape
    blk = lambda cols: pl.BlockSpec((batch, rows, cols), lambda i: (0, i, 0))
    s_shape = (batch, GLA_HEADS, GLA_DK, GLA_DV)
    return pl.pallas_call(
        _gla_scan_body,
        grid=(length // rows,),
        in_specs=[blk(kw), blk(kw), blk(D_MODEL), blk(kw), blk(D_MODEL), _resident((1, GLA_DV))],
        out_specs=(blk(D_MODEL), pl.BlockSpec(s_shape, lambda i: (0, 0, 0, 0))),
        out_shape=(jax.ShapeDtypeStruct((batch, length, D_MODEL), BF16), jax.ShapeDtypeStruct(s_shape, F32)),
        scratch_shapes=[pltpu.VMEM((batch, GLA_HEADS, GLA_DV, GLA_DK), F32)],
        compiler_params=_params("arbitrary"),
        name="gla_scan",
    )(q, k, v, g, r, gn)


def _gla_sample_proj_body(x_ref, n1_ref, shift_ref, scale_ref, wq_ref, wk_ref, wv_ref, wa1_ref, wa2_ref, ba_ref,
                          wr_ref, q_ref, k_ref, v_ref, g_ref, r_ref):
    hb = _prenorm(x_ref[...], n1_ref[...], shift_ref[...], scale_ref[...]).astype(BF16)
    q_ref[...] = jnp.dot(hb, wq_ref[...], preferred_element_type=F32).T * (GLA_DK ** -0.5)
    k_ref[...] = jnp.dot(hb, wk_ref[...], preferred_element_type=F32).T
    v_ref[...] = jnp.dot(hb, wv_ref[...], preferred_element_type=F32)
    g_ref[...] = _gla_gate(hb, wa1_ref, wa2_ref, ba_ref[...]).T
    r_ref[...] = _silu(jnp.dot(hb, wr_ref[...], preferred_element_type=F32))


def _gla_sample_proj_call(x, n1, mods, w):
    bd = x.shape[0]
    kw = GLA_HEADS * GLA_DK
    rank = w["wa1"].shape[1]
    mod = lambda c: pl.BlockSpec((bd, D_MODEL), lambda i: (0, c), pipeline_mode=pl.Buffered(1))
    full = lambda *s: pl.BlockSpec(s, lambda i: (0,) * len(s))
    return pl.pallas_call(
        _gla_sample_proj_body,
        grid=(1,),
        in_specs=[_resident((bd, D_MODEL)), _resident((1, D_MODEL)), mod(0), mod(1),
                  _resident((D_MODEL, kw)), _resident((D_MODEL, kw)), _resident((D_MODEL, D_MODEL)),
                  _resident((D_MODEL, rank)), _resident((rank, kw)), _resident((1, kw)), _resident((D_MODEL, D_MODEL))],
        out_specs=(full(kw, bd), full(kw, bd), full(bd, D_MODEL), full(kw, bd), full(bd, D_MODEL)),
        out_shape=(jax.ShapeDtypeStruct((kw, bd), F32), jax.ShapeDtypeStruct((kw, bd), F32),
                   jax.ShapeDtypeStruct((bd, D_MODEL), F32), jax.ShapeDtypeStruct((kw, bd), F32),
                   jax.ShapeDtypeStruct((bd, D_MODEL), F32)),
        compiler_params=_params("arbitrary"),
        name="gla_sample_proj",
    )(x, n1, mods, mods, w["wq"], w["wk"], w["wv"], w["wa1"], w["wa2"], w["ba"], w["wr"])


def _gla_decode_body(q_ref, k_ref, g_ref, v_ref, r_ref, gn_ref, s_ref, y_ref, s_out_ref):
    b = pl.program_id(0)
    n = q_ref.shape[1]
    sel = (lax.broadcasted_iota(jnp.int32, (n, n), 0) == b).astype(BF16)
    wide = lambda col: jnp.concatenate([col, col], axis=-1).reshape(GLA_HEADS, GLA_DK, GLA_DV)
    qcol = wide(_lane_column(q_ref[...], sel))
    kcol = wide(_lane_column(k_ref[...], sel))
    decay = wide(jnp.exp(_lane_column(g_ref[...], sel)))
    gn = gn_ref[...]
    for h in range(GLA_HEADS):
        vs = slice(h * GLA_DV, (h + 1) * GLA_DV)
        s_new = s_ref[h] * decay[h] + kcol[h] * v_ref[:, vs]
        s_out_ref[h] = s_new
        o = jnp.sum(qcol[h] * s_new, axis=0, keepdims=True)
        y_ref[:, vs] = _gla_out_norm(o, gn, r_ref[:, vs])


def _gla_decode_call(q_t, k_t, g_t, v, r, gn, state):
    kw, bd = q_t.shape
    assert bd == V7X_LANES, "the sample batch is laid out on the 128 lanes"
    full = lambda *s: pl.BlockSpec(s, lambda i: (0,) * len(s))
    row = pl.BlockSpec((None, 1, D_MODEL), lambda i: (i, 0, 0))
    st = pl.BlockSpec((None, GLA_HEADS, GLA_DK, GLA_DV), lambda i: (i, 0, 0, 0))
    return pl.pallas_call(
        _gla_decode_body,
        grid=(bd,),
        in_specs=[full(kw, bd), full(kw, bd), full(kw, bd), row, row, full(1, GLA_DV), st],
        out_specs=(row, st),
        out_shape=(jax.ShapeDtypeStruct((bd, 1, D_MODEL), F32), jax.ShapeDtypeStruct(state.shape, F32)),
        compiler_params=_params("arbitrary"),
        name="gla_decode",
    )(q_t, k_t, g_t, v, r, gn, state)


def kernel(x_prompt, x_sample, cache_k, cache_v, cache_logf, state_pool, state_gla, page_table, c_prompt, c_sample, norm1, norm2, ada_w, ada_b, fox_wq, fox_wk, fox_wv, fox_wf, fox_bf, fox_qn, fox_kn, fox_wo, pool_w, pool_scale, gla_wq, gla_wk, gla_wv, gla_wa1, gla_wa2, gla_ba, gla_wr, gla_gn, gla_wo, ffn_wg, ffn_wu, ffn_wd):
    batch, length, d = x_prompt.shape
    bd = x_sample.shape[0]
    depth = norm1.shape[0]
    assert d == D_MODEL and x_sample.shape[1] == 1

    c_all = jnp.concatenate([c_prompt, c_sample], axis=0)
    c_all = jnp.pad(c_all, ((0, -c_all.shape[0] % 8), (0, 0)))
    mods = _ada_call(c_all, ada_w, ada_b)

    ck_t = jnp.transpose(cache_k, (0, 1, 3, 4, 2))
    cv_t = jnp.transpose(cache_v, (0, 1, 3, 4, 2))
    clf_t = jnp.transpose(cache_logf, (0, 1, 3, 2))

    xp = x_prompt
    xs = x_sample.reshape(1, bd, d)
    n_fox = (depth + N_MIXERS - 1) // N_MIXERS
    stack_shape = (n_fox, batch, FOX_HEADS, FOX_HEAD_DIM, length)
    kv_stacks = (jnp.zeros(stack_shape, F32), jnp.ones(stack_shape, F32))
    outs = {k: [] for k in ("lf_p", "k_s", "v_s", "lf_s", "pool_p", "pool_s", "gla_p", "gla_s")}
    for i in range(depth):
        kind, j = i % N_MIXERS, i // N_MIXERS
        rp = _Rows(mods[i, :batch], batch, length, ROW_BLOCK)
        rsm = _Rows(mods[i, batch:batch + bd], 1, bd, bd, per_row=True)
        n1, n2 = norm1[i][None], norm2[i][None]
        ffn = (ffn_wg[i].astype(BF16), ffn_wu[i].astype(BF16), ffn_wd[i].astype(BF16))
        if kind == 0:
            w = dict(wq_t=fox_wq[j].T.astype(BF16), wk_t=fox_wk[j].T.astype(BF16), wv_t=fox_wv[j].T.astype(BF16),
                     wf_t=fox_wf[j].T.astype(BF16), bf=fox_bf[j][:, None], qn=fox_qn[j][:, None], kn=fox_kn[j][:, None])
            wo = fox_wo[j].astype(BF16)
            qs_t, ks_t, vs_t, lfs_t = _fox_sample_proj_call(xs, n1, rsm, w)
            ys = _fox_decode_call(page_table, ck_t, cv_t, clf_t, j, qs_t[0], ks_t[0], vs_t[0], lfs_t[0])
            qa, kr, *kv_stacks, lf_t = _fox_prompt_proj_call(xp, n1, rp, w, j, kv_stacks)
            yp = _fox_attn_call(qa, kr, kv_stacks[1], j)
            outs["lf_p"].append(jnp.transpose(lf_t, (0, 2, 1)))
            outs["k_s"].append(jnp.transpose(ks_t, (3, 0, 1, 2)))
            outs["v_s"].append(jnp.transpose(vs_t, (3, 0, 1, 2)))
            outs["lf_s"].append(jnp.transpose(lfs_t, (2, 0, 1)))
            xp = _out_ffn_call(xp, yp, wo, n2, rp, ffn)
            xs = _out_ffn_call(xs, ys[None], wo, n2, rsm, ffn)
        elif kind == 1:
            pw, ps = pool_w[j].astype(BF16), pool_scale[j][None]
            xp, tail = _pool_ffn_call(xp, n1, pw, ps, n2, _Rows(mods[i, :batch], batch, length, POOL_ROW_BLOCK), ffn)
            st_t = jnp.transpose(state_pool[j], (1, 0, 2))
            xs2, st_new = _pool_sample_call(xs[0], st_t, n1, pw, ps, n2, rsm.mods, ffn)
            xs = xs2[None]
            outs["pool_p"].append(tail[:, POOL_HALO - POOL_STATE:])
            outs["pool_s"].append(jnp.transpose(st_new, (1, 0, 2)))
        else:
            w = dict(wq=gla_wq[j].astype(BF16), wk=gla_wk[j].astype(BF16), wv=gla_wv[j].astype(BF16),
                     wa1=gla_wa1[j].astype(BF16), wa2=gla_wa2[j].astype(BF16), ba=gla_ba[j][None],
                     wr=gla_wr[j].astype(BF16))
            wo, gn = gla_wo[j].astype(BF16), gla_gn[j][None]
            q, k, v, g, r = _gla_proj_call(xp, n1, rp, w)
            yp, s_fin = _gla_scan_call(q, k, v, g, r, gn, 4 * GLA_CHUNK)
            q_t, k_t, vs_, g_t, r_s = _gla_sample_proj_call(xs[0], n1, rsm.mods, w)
            ys, s_new = _gla_decode_call(q_t, k_t, g_t, vs_[:, None], r_s[:, None], gn, state_gla[j])
            outs["gla_p"].append(s_fin)
            outs["gla_s"].append(s_new)
            xp = _out_ffn_call(xp, yp, wo, n2, rp, ffn)
            xs = _out_ffn_call(xs, ys.reshape(1, bd, d).astype(BF16), wo, n2, rsm, ffn)

    stack = lambda name: jnp.stack(outs[name])
    return (xp, xs.reshape(bd, 1, d),
            jnp.transpose(kv_stacks[0], (0, 1, 4, 2, 3)), jnp.transpose(kv_stacks[1], (0, 1, 4, 2, 3)), stack("lf_p"),
            stack("k_s"), stack("v_s"), stack("lf_s"),
            stack("pool_p"), stack("pool_s"), stack("gla_p"), stack("gla_s"))
```

```python
import functools
import math

import jax
import jax.numpy as jnp
import numpy as np
from jax import lax
from jax.experimental import pallas as pl
from jax.experimental.pallas import tpu as pltpu

F32, BF16 = jnp.float32, jnp.bfloat16

D_MODEL = 1024
N_MIXERS = 3
PAGE_SIZE = 128
FOX_HEADS = 16
FOX_HEAD_DIM = 64
FOX_SCALE = FOX_HEAD_DIM ** -0.5
LOG2E = math.log2(math.e)
FOX_AUG_ROWS = 16
POOL_WINDOWS = (2, 4, 8, 16)
POOL_GROUP_DIM = D_MODEL // len(POOL_WINDOWS)
POOL_STATE = max(POOL_WINDOWS) - 1
POOL_HALO = 16
GLA_HEADS = 4
GLA_DK = 128
GLA_DV = 256
GLA_TAU = 16.0
GLA_CHUNK = 64
D_FF = 2816
EPS = 1e-6

V7X_LANES = 128
V7X_VMEM_LIMIT_BYTES = 56 * 1024 * 1024

ROW_BLOCK = 512
POOL_ROW_BLOCK = 256
ATTN_BLOCK = 512
ATTN_HEADS = 16
V7X_MXU_DIM = 256
FF_CHUNK = 6 * V7X_MXU_DIM
ADA_COLS = 1536

NT_DIMS = (((1,), (1,)), ((), ()))
TN_DIMS = (((0,), (0,)), ((), ()))


def _params(*sem):
    return pltpu.CompilerParams(dimension_semantics=sem, vmem_limit_bytes=V7X_VMEM_LIMIT_BYTES)


def _resident(shape):
    zeros = (0,) * len(shape)
    return pl.BlockSpec(shape, lambda *_: zeros, pipeline_mode=pl.Buffered(1))


def _silu(x):
    return x * jax.nn.sigmoid(x)


def _log_sigmoid(x):
    return jnp.minimum(x, 0.0) - jnp.log1p(jnp.exp(-jnp.abs(x)))


def _rms(x, g):
    return x * lax.rsqrt(jnp.mean(x * x, axis=-1, keepdims=True) + EPS) * g


def _prenorm(x, g, shift, scale):
    return _rms(x, g) * (1.0 + scale) + shift


def _split3(x):
    hi = x.astype(BF16)
    r1 = x - hi.astype(F32)
    mid = r1.astype(BF16)
    lo = (r1 - mid.astype(F32)).astype(BF16)
    return hi, mid, lo


def _dot3(x, m01):
    hi, mid, lo = _split3(x)
    return (jnp.dot(hi, m01, preferred_element_type=F32)
            + jnp.dot(mid, m01, preferred_element_type=F32)
            + jnp.dot(lo, m01, preferred_element_type=F32))


def _dot3_left(m01, x):
    hi, mid, lo = _split3(x)
    return (jnp.dot(m01, hi, preferred_element_type=F32)
            + jnp.dot(m01, mid, preferred_element_type=F32)
            + jnp.dot(m01, lo, preferred_element_type=F32))


def _ffn_tail(x1, n2, shift2, scale2, gate2, wg_ref, wu_ref, wd_ref):
    f = _prenorm(x1, n2, shift2, scale2).astype(BF16)
    acc = jnp.zeros(x1.shape, F32)
    for c0 in range(0, D_FF, FF_CHUNK):
        c1 = min(c0 + FF_CHUNK, D_FF)
        g = jnp.dot(f, wg_ref[:, c0:c1], preferred_element_type=F32)
        u = jnp.dot(f, wu_ref[:, c0:c1], preferred_element_type=F32)
        a = (_silu(g) * u).astype(BF16)
        acc = acc + jnp.dot(a, wd_ref[c0:c1, :], preferred_element_type=F32)
    return x1 + gate2 * acc


def _ada_body(c_ref, w_ref, b_ref, o_ref):
    a = _silu(c_ref[...]).astype(BF16)
    o_ref[...] = jnp.dot(a, w_ref[...].astype(BF16), preferred_element_type=F32) + b_ref[...]


def _ada_call(c_all, ada_w, ada_b):
    depth, d, n = ada_w.shape
    rows = c_all.shape[0]
    return pl.pallas_call(
        _ada_body,
        grid=(depth, n // ADA_COLS),
        in_specs=[
            pl.BlockSpec((rows, d), lambda i, j: (0, 0)),
            pl.BlockSpec((None, d, ADA_COLS), lambda i, j: (i, 0, j)),
            pl.BlockSpec((None, 1, ADA_COLS), lambda i, j: (i, 0, j)),
        ],
        out_specs=pl.BlockSpec((None, rows, ADA_COLS), lambda i, j: (i, 0, j)),
        out_shape=jax.ShapeDtypeStruct((depth, rows, n), F32),
        compiler_params=_params("arbitrary", "arbitrary"),
        name="ada_mods",
    )(c_all, ada_w, ada_b.reshape(depth, 1, n))


class _Rows:
    def __init__(self, mods, batch, length, tm, per_row=False):
        self.batch, self.length, self.tm = batch, length, tm
        self.grid = (batch, length // tm)
        self.per_row = per_row
        self.mods = mods if per_row else mods.reshape(batch, 6, 1, D_MODEL)

    def rows(self, cols):
        return pl.BlockSpec((None, self.tm, cols), lambda b, i: (b, i, 0))

    def mod(self, c):
        if self.per_row:
            return pl.BlockSpec((self.tm, D_MODEL), lambda b, i: (i, c))
        return pl.BlockSpec((None, None, 1, D_MODEL), lambda b, i: (b, c, 0, 0))

    def shape(self, cols, dtype=F32):
        return jax.ShapeDtypeStruct((self.batch, self.length, cols), dtype)


def _head_norm(y, gain_col):
    ms = jnp.mean(y * y, axis=1, keepdims=True)
    return y * lax.rsqrt(ms + EPS) * gain_col[None, :, :]


FOX_PROJ_INPUTS = 12


def _fox_proj_body(*refs, with_bias_rows):
    _fox_proj_compute(pl.program_id(1) == 0, *refs, with_bias_rows=with_bias_rows)


def _fox_proj_compute(first_block, x_ref, n1_ref, shift_ref, scale_ref, wq_ref, wk_ref, wv_ref, wf_ref, bf_ref,
                      qn_ref, kn_ref, tri_ref, *out_refs, with_bias_rows):
    tm = x_ref.shape[0]
    hb = _prenorm(x_ref[...], n1_ref[...], shift_ref[...], scale_ref[...]).astype(BF16)

    def proj_t(w_ref):
        return lax.dot_general(w_ref[...], hb, NT_DIMS, preferred_element_type=F32)

    q = _head_norm(proj_t(wq_ref).reshape(FOX_HEADS, FOX_HEAD_DIM, tm), qn_ref[...])
    k = _head_norm(proj_t(wk_ref).reshape(FOX_HEADS, FOX_HEAD_DIM, tm), kn_ref[...])
    v = proj_t(wv_ref).reshape(FOX_HEADS, FOX_HEAD_DIM, tm)
    lf = _log_sigmoid(proj_t(wf_ref) + bf_ref[...])

    if not with_bias_rows:
        q_ref, k_ref, v_ref, lf_ref = out_refs
        q_ref[...], k_ref[...], v_ref[...], lf_ref[...] = q, k, v, lf
        return

    qa_ref, kr_ref, k_ref, v_ref, lf_ref, carry_ref = out_refs
    k_ref[...], v_ref[...], lf_ref[...] = k, v, lf

    @pl.when(first_block)
    def _():
        carry_ref[...] = jnp.zeros_like(carry_ref)

    f_cum = _dot3(lf, tri_ref[...]) + carry_ref[:, 0:1]
    carry_ref[...] = jnp.broadcast_to(f_cum[:, tm - 1:tm], carry_ref.shape)

    hi, mid, lo = (p.astype(F32)[:, None, :] for p in _split3(f_cum * LOG2E))
    r = lax.broadcasted_iota(jnp.int32, (FOX_HEADS, FOX_AUG_ROWS, tm), 1)
    q_rows = jnp.where(r == 0, hi, jnp.where(r == 1, mid, jnp.where(r == 2, lo, jnp.where(r < 6, 1.0, 0.0))))
    k_rows = jnp.where(r < 3, 1.0, jnp.where(r == 3, -hi, jnp.where(r == 4, -mid, jnp.where(r == 5, -lo, 0.0))))
    pad =jnp.zeros((FOX_HEADS, 2 * FOX_HEAD_DIM - FOX_HEAD_DIM - FOX_AUG_ROWS, tm), BF16)
    qa_ref[:, 0:FOX_HEAD_DIM, :] = (q * (FOX_SCALE * LOG2E)).astype(BF16)
    qa_ref[:, FOX_HEAD_DIM:FOX_HEAD_DIM + FOX_AUG_ROWS, :] = q_rows.astype(BF16)
    qa_ref[:, FOX_HEAD_DIM + FOX_AUG_ROWS:, :] = pad
    ka = jnp.concatenate([k.astype(BF16), k_rows.astype(BF16), pad], axis=1)
    for h in range(FOX_HEADS):
        kr_ref[h] = ka[h].T


def _fox_prompt_proj_body(*refs):
    refs = refs[:FOX_PROJ_INPUTS] + refs[FOX_PROJ_INPUTS + 2:]
    _fox_proj_compute(pl.program_id(1) == 0, *refs, with_bias_rows=True)


def _fox_prompt_proj_call(x, n1, rs, w, layer, stacks):
    batch, length, tm = rs.batch, rs.length, rs.tm
    pair_rows = 2 * FOX_HEAD_DIM
    st_shape = jax.ShapeDtypeStruct(stacks[0].shape, F32)
    st_spec = pl.BlockSpec((None, None, FOX_HEADS, FOX_HEAD_DIM, tm), lambda b, i: (layer, b, 0, 0, i))
    tri = jnp.triu(jnp.ones((tm, tm), BF16))
    operands = (x, n1, rs.mods, rs.mods, w["wq_t"], w["wk_t"], w["wv_t"], w["wf_t"], w["bf"], w["qn"], w["kn"], tri)
    assert len(operands) == FOX_PROJ_INPUTS
    return pl.pallas_call(
        _fox_prompt_proj_body,
        grid=rs.grid,
        in_specs=[rs.rows(D_MODEL), _resident((1, D_MODEL)), rs.mod(0), rs.mod(1),
                  _resident((D_MODEL, D_MODEL)), _resident((D_MODEL, D_MODEL)), _resident((D_MODEL, D_MODEL)),
                  _resident((FOX_HEADS, D_MODEL)), _resident((FOX_HEADS, 1)),
                  _resident((FOX_HEAD_DIM, 1)), _resident((FOX_HEAD_DIM, 1)), _resident((tm, tm)),
                  pl.BlockSpec(memory_space=pl.ANY), pl.BlockSpec(memory_space=pl.ANY)],
        out_specs=(pl.BlockSpec((None, FOX_HEADS, pair_rows, tm), lambda b, i: (b, 0, 0, i)),
                   pl.BlockSpec((None, FOX_HEADS, tm, pair_rows), lambda b, i: (b, 0, i, 0)),
                   st_spec, st_spec,
                   pl.BlockSpec((None, FOX_HEADS, tm), lambda b, i: (b, 0, i))),
        out_shape=(jax.ShapeDtypeStruct((batch, FOX_HEADS, pair_rows, length), BF16),
                   jax.ShapeDtypeStruct((batch, FOX_HEADS, length, pair_rows), BF16),
                   st_shape, st_shape,
                   jax.ShapeDtypeStruct((batch, FOX_HEADS, length), F32)),
        scratch_shapes=[pltpu.VMEM((FOX_HEADS, V7X_LANES), F32)],
        input_output_aliases={FOX_PROJ_INPUTS: 2, FOX_PROJ_INPUTS + 1: 3},
        compiler_params=_params("arbitrary", "arbitrary"),
        name="fox_proj_bias",
    )(*operands, *stacks)


def _fox_sample_proj_call(x, n1, rs, w):
    batch, length, tm = rs.batch, rs.length, rs.tm
    hshape = jax.ShapeDtypeStruct((batch, FOX_HEADS, FOX_HEAD_DIM, length), F32)
    hspec = pl.BlockSpec((None, FOX_HEADS, FOX_HEAD_DIM, tm), lambda b, i: (b, 0, 0, i))
    tri = jnp.triu(jnp.ones((tm, tm), BF16))
    return pl.pallas_call(
        functools.partial(_fox_proj_body, with_bias_rows=False),
        grid=rs.grid,
        in_specs=[rs.rows(D_MODEL), _resident((1, D_MODEL)), rs.mod(0), rs.mod(1),
                  _resident((D_MODEL, D_MODEL)), _resident((D_MODEL, D_MODEL)), _resident((D_MODEL, D_MODEL)),
                  _resident((FOX_HEADS, D_MODEL)), _resident((FOX_HEADS, 1)),
                  _resident((FOX_HEAD_DIM, 1)), _resident((FOX_HEAD_DIM, 1)), _resident((tm, tm))],
        out_specs=(hspec,) * 3 + (pl.BlockSpec((None, FOX_HEADS, tm), lambda b, i: (b, 0, i)),),
        out_shape=(hshape,) * 3 + (jax.ShapeDtypeStruct((batch, FOX_HEADS, length), F32),),
        compiler_params=_params("arbitrary", "arbitrary"),
        name="fox_proj",
    )(x, n1, rs.mods, rs.mods, w["wq_t"], w["wk_t"], w["wv_t"], w["wf_t"], w["bf"], w["qn"], w["kn"], tri)


def _fox_attn_body(qi_tab, ki_tab, q_ref, k_ref, v_ref, o_ref, m_ref, acc_ref):
    t = pl.program_id(2)
    qi, ki = qi_tab[t], ki_tab[t]
    nh, tk, tq = k_ref.shape[0], k_ref.shape[1], q_ref.shape[2]
    hd = FOX_HEAD_DIM

    @pl.when(ki == 0)
    def _():
        m_ref[...] = jnp.full_like(m_ref, -jnp.inf)
        acc_ref[...] = jnp.zeros_like(acc_ref)

    def step(diagonal):
        scores = [jnp.dot(k_ref[hh], q_ref[hh], preferred_element_type=F32) for hh in range(nh)]
        ones_row = (lax.broadcasted_iota(jnp.int32, (hd, tk), 0) == 0).astype(BF16)
        for hh in range(nh):
            s = scores[hh]
            if diagonal:
                key = lax.broadcasted_iota(jnp.int32, (tk, tq), 0)
                qry = lax.broadcasted_iota(jnp.int32, (tk, tq), 1)
                s = jnp.where(key > qry, -jnp.inf, s)
            m_prev = m_ref[hh]
            m_new = jnp.maximum(m_prev, jnp.max(s, axis=0, keepdims=True))
            p = jnp.exp2(s - m_new).astype(BF16)
            m_ref[hh] = m_new
            vb = v_ref[hh * hd:(hh + 1) * hd, :].astype(BF16)
            v_aug = jnp.concatenate([vb, ones_row] if hh % 2 == 0 else [ones_row, vb], axis=0)
            acc_ref[hh] = (acc_ref[hh] * jnp.exp2(m_prev - m_new)
                           + jnp.dot(v_aug, p, preferred_element_type=F32))

    @pl.when(ki < qi)
    def _():
        step(False)

    @pl.when(ki == qi)
    def _():
        step(True)
        first = lax.broadcasted_iota(jnp.int32, (2 * hd, tq), 0) < hd
        for pair in range(nh // 2):
            a0, a1 = acc_ref[2 * pair], acc_ref[2 * pair + 1]
            o_t = jnp.where(first, a0 / a0[hd:hd + 1, :], a1 / a1[0:1, :])
            o_ref[:, pair * 2 * hd:(pair + 1) * 2 * hd] = o_t.T.astype(o_ref.dtype)


def _fox_attn_call(qa, kr, v_stack, layer):
    batch, _, rows, length = qa.shape
    tq = tk = ATTN_BLOCK
    nh = ATTN_HEADS
    groups = FOX_HEADS // nh
    nq = length // tq
    pairs = [(i, j) for i in range(nq) for j in range(i + 1)]
    qi_tab = jnp.asarray([p[0] for p in pairs], jnp.int32)
    ki_tab = jnp.asarray([p[1] for p in pairs], jnp.int32)
    qa = qa.reshape(batch, groups, nh, rows, length)
    kr = kr.reshape(batch, groups, nh, length, rows)
    v_stack = v_stack.reshape(v_stack.shape[0], batch, groups, nh * FOX_HEAD_DIM, length)
    return pl.pallas_call(
        _fox_attn_body,
        grid_spec=pltpu.PrefetchScalarGridSpec(
            num_scalar_prefetch=2,
            grid=(batch, groups, len(pairs)),
            in_specs=[
                pl.BlockSpec((None, None, nh, rows, tq), lambda b, h, t, qt, kt: (b, h, 0, 0, qt[t])),
                pl.BlockSpec((None, None, nh, tk, rows), lambda b, h, t, qt, kt: (b, h, 0, kt[t], 0)),
                pl.BlockSpec((None, None, None, nh * FOX_HEAD_DIM, tk), lambda b, h, t, qt, kt: (layer, b, h, 0, kt[t])),
            ],
            out_specs=pl.BlockSpec((None, tq, nh * FOX_HEAD_DIM), lambda b, h, t, qt, kt: (b, qt[t], h)),
            scratch_shapes=[pltpu.VMEM((nh, 1, tq), F32), pltpu.VMEM((nh, rows, tq), F32)],
        ),
        out_shape=jax.ShapeDtypeStruct((batch, length, D_MODEL), BF16),
        compiler_params=_params("arbitrary", "arbitrary", "arbitrary"),
        name="fox_attn",
    )(qi_tab, ki_tab, qa, kr, v_stack)


def _lane_column(x_t, sel):
    return _dot3(x_t, sel)


def _fox_decode_compute(*refs):
    n_pages = (len(refs) - 6) // 3
    ck_refs, cv_refs, clf_refs = refs[:n_pages], refs[n_pages:2 * n_pages], refs[2 * n_pages:3 * n_pages]
    q_ref, k_ref, v_ref, lf_ref, o_ref, ot_ref = refs[3 * n_pages:]
    b = pl.program_id(0)
    hs = (FOX_HEADS, FOX_HEAD_DIM, PAGE_SIZE)
    rows = FOX_HEADS * FOX_HEAD_DIM
    lane = lax.broadcasted_iota(jnp.int32, (FOX_HEADS, PAGE_SIZE), 1)

    @pl.when(b == 0)
    def _():
        ot_ref[...] = jnp.zeros_like(ot_ref)

    sel = (lax.broadcasted_iota(jnp.int32, (PAGE_SIZE, PAGE_SIZE), 0) == b).astype(BF16)
    qcol = _lane_column(q_ref[...].reshape(rows, PAGE_SIZE), sel).reshape(hs) * FOX_SCALE
    kcol = _lane_column(k_ref[...].reshape(rows, PAGE_SIZE), sel).reshape(hs)
    vcol = _lane_column(v_ref[...].reshape(rows, PAGE_SIZE), sel).reshape(hs)
    s_self = jnp.sum(qcol * kcol, axis=1)

    lf_all = jnp.concatenate([r[...] for r in clf_refs], axis=0)
    after = (lax.broadcasted_iota(jnp.int32, (PAGE_SIZE, PAGE_SIZE), 0)
             > lax.broadcasted_iota(jnp.int32, (PAGE_SIZE, PAGE_SIZE), 1)).astype(BF16)
    suffix_all = _dot3(lf_all, after)
    later = _lane_column(lf_ref[...], sel)
    scores = [None] * n_pages
    for p in reversed(range(n_pages)):
        pg = slice(p * FOX_HEADS, (p + 1) * FOX_HEADS)
        scores[p] = jnp.sum(qcol * ck_refs[p][...], axis=1) + later + suffix_all[pg]
        later = later + jnp.broadcast_to(suffix_all[pg, 0:1] + lf_all[pg, 0:1], lane.shape)

    m = s_self
    for s in scores:
        m = jnp.maximum(m, s)
    m = jnp.max(m, axis=-1, keepdims=True)
    p_self = jnp.exp(s_self - m)
    den = jnp.where(lane == 0, p_self, 0.0)
    acc = jnp.where(lane[:, None, :] == 0, vcol * p_self[:, None, :], 0.0)
    for p in range(n_pages):
        pr = jnp.exp(scores[p] - m)
        den = den + pr
        acc = acc + cv_refs[p][...] * pr[:, None, :]
    num = jnp.sum(acc, axis=-1, keepdims=True)
    den = jnp.sum(den, axis=-1, keepdims=True)[:, None, :]
    o = jnp.broadcast_to(num / den, hs)
    ot_ref[...] = jnp.where(lane[:, None, :] == b, o, ot_ref[...])

    @pl.when(b == pl.num_programs(0) - 1)
    def _():
        o_ref[...] = ot_ref[...].reshape(rows, PAGE_SIZE).T.astype(o_ref.dtype)


def _fox_decode_body(pt_ref, *refs):
    _fox_decode_compute(*refs)


def _fox_decode_call(page_table, ck_t, cv_t, clf_t, j, q_t, k_t, v_t, lf_t):
    n_seq, n_pages = page_table.shape
    assert n_seq == PAGE_SIZE, "the sample batch is laid out on the 128 lanes"
    hs = (FOX_HEADS, FOX_HEAD_DIM, PAGE_SIZE)
    kv_page = lambda p: pl.BlockSpec((None, None) + hs, lambda b, pt: (j, pt[b, p], 0, 0, 0))
    lf_page = lambda p: pl.BlockSpec((None, None, FOX_HEADS, PAGE_SIZE), lambda b, pt: (j, pt[b, p], 0, 0))
    whole = lambda shape: pl.BlockSpec(shape, lambda b, pt: (0,) * len(shape))
    pages = range(n_pages)
    return pl.pallas_call(
        _fox_decode_body,
        grid_spec=pltpu.PrefetchScalarGridSpec(
            num_scalar_prefetch=1,
            grid=(n_seq,),
            in_specs=([kv_page(p) for p in pages] + [kv_page(p) for p in pages] + [lf_page(p) for p in pages]
                      + [whole(hs), whole(hs), whole(hs), whole((FOX_HEADS, PAGE_SIZE))]),
            out_specs=whole((n_seq, D_MODEL)),
            scratch_shapes=[pltpu.VMEM(hs, F32)],
        ),
        out_shape=jax.ShapeDtypeStruct((n_seq, D_MODEL), BF16),
        compiler_params=_params("arbitrary"),
        name="fox_decode",
    )(page_table, *([ck_t] * n_pages), *([cv_t] * n_pages), *([clf_t] * n_pages), q_t, k_t, v_t, lf_t)


def _out_ffn_body(x_ref, y_ref, wo_ref, gate1_ref, n2_ref, shift2_ref, scale2_ref, gate2_ref,
                  wg_ref, wu_ref, wd_ref, o_ref):
    mix = jnp.dot(y_ref[...], wo_ref[...], preferred_element_type=F32)
    x1 = x_ref[...] + gate1_ref[...] * mix
    o_ref[...] = _ffn_tail(x1, n2_ref[...], shift2_ref[...], scale2_ref[...], gate2_ref[...], wg_ref, wu_ref, wd_ref)


def _ffn_specs(layer):
    one = lambda rows, cols: pl.BlockSpec((None, rows, cols), lambda *_: (layer, 0, 0), pipeline_mode=pl.Buffered(1))
    return [one(D_MODEL, D_FF), one(D_MODEL, D_FF), one(D_FF, D_MODEL)]


def _out_ffn_call(x, y, wo, n2, rs, ffn):
    return pl.pallas_call(
        _out_ffn_body,
        grid=rs.grid,
        in_specs=[rs.rows(D_MODEL), rs.rows(D_MODEL), _resident((D_MODEL, D_MODEL)), rs.mod(2),
                  _resident((1, D_MODEL)), rs.mod(3), rs.mod(4), rs.mod(5)] + _ffn_specs(ffn[0]),
        out_specs=rs.rows(D_MODEL),
        out_shape=rs.shape(D_MODEL),
        compiler_params=_params("arbitrary", "arbitrary"),
        name="out_ffn",
    )(x, y, wo, rs.mods, n2, rs.mods, rs.mods, rs.mods, *ffn[1:])


def _pool_groups(window_sums, h, inv_cnt, pw_ref, ps_ref):
    outs = []
    for g in range(len(POOL_WINDOWS)):
        sl = slice(g * POOL_GROUP_DIM, (g + 1) * POOL_GROUP_DIM)
        d = (window_sums[g] * inv_cnt[g] - h[:, sl]).astype(BF16)
        outs.append(jnp.dot(d, pw_ref[g], preferred_element_type=F32))
    return jnp.concatenate(outs, axis=-1) * ps_ref[...]


def _pool_ffn_body(x_ref, n1_ref, shift1_ref, scale1_ref, gate1_ref, pw_ref, ps_ref,
                   n2_ref, shift2_ref, scale2_ref, gate2_ref, wg_ref, wu_ref, wd_ref,
                   o_ref, tail_ref, ext_ref):
    tm = x_ref.shape[0]
    i = pl.program_id(1)
    x = x_ref[...]
    h = _prenorm(x, n1_ref[...], shift1_ref[...], scale1_ref[...])

    @pl.when(i == 0)
    def _():
        ext_ref[0:POOL_HALO, :] = jnp.zeros((POOL_HALO, D_MODEL), F32)

    ext_ref[POOL_HALO:, :] = h
    pos = i * tm + lax.broadcasted_iota(jnp.int32, (tm, 1), 0)
    sums, inv_cnt = [], []
    run, back = h, 1
    for g, win in enumerate(POOL_WINDOWS):
        lo = g * POOL_GROUP_DIM
        while back < win:
            run = run + ext_ref[POOL_HALO - back:POOL_HALO - back + tm, lo:]
            back += 1
        sums.append(run[:, :POOL_GROUP_DIM])
        if g + 1 < len(POOL_WINDOWS):
            run = run[:, POOL_GROUP_DIM:]
        inv_cnt.append(1.0 / jnp.minimum(win, pos + 1).astype(F32))
    y = _pool_groups(sums, h, inv_cnt, pw_ref, ps_ref)
    x1 = x + gate1_ref[...] * y
    o_ref[...] = _ffn_tail(x1, n2_ref[...], shift2_ref[...], scale2_ref[...], gate2_ref[...], wg_ref, wu_ref, wd_ref)
    halo = ext_ref[tm:tm + POOL_HALO, :]
    ext_ref[0:POOL_HALO, :] = halo
    tail_ref[...] = halo


def _pool_ffn_call(x, n1, pool_w, pool_scale, n2, rs, ffn):
    g = len(POOL_WINDOWS)
    return pl.pallas_call(
        _pool_ffn_body,
        grid=rs.grid,
        in_specs=[rs.rows(D_MODEL), _resident((1, D_MODEL)), rs.mod(0), rs.mod(1), rs.mod(2),
                  _resident((g, POOL_GROUP_DIM, POOL_GROUP_DIM)), _resident((1, D_MODEL)),
                  _resident((1, D_MODEL)), rs.mod(3), rs.mod(4), rs.mod(5)] + _ffn_specs(ffn[0]),
        out_specs=(rs.rows(D_MODEL), pl.BlockSpec((None, POOL_HALO, D_MODEL), lambda b, i: (b, 0, 0))),
        out_shape=(rs.shape(D_MODEL), jax.ShapeDtypeStruct((rs.batch, POOL_HALO, D_MODEL), F32)),
        scratch_shapes=[pltpu.VMEM((POOL_HALO + rs.tm, D_MODEL), F32)],
        compiler_params=_params("arbitrary", "arbitrary"),
        name="pool_ffn",
    )(x, n1, rs.mods, rs.mods, rs.mods, pool_w, pool_scale, n2, rs.mods, rs.mods, rs.mods, *ffn[1:])


def _pool_sample_body(x_ref, st_ref, n1_ref, shift1_ref, scale1_ref, gate1_ref, pw_ref, ps_ref,
                      n2_ref, shift2_ref, scale2_ref, gate2_ref, wg_ref, wu_ref, wd_ref, o_ref, st_out_ref):
    x = x_ref[...]
    h = _prenorm(x, n1_ref[...], shift1_ref[...], scale1_ref[...])
    sums, inv_cnt = [], []
    run = h
    back = 1
    for g, win in enumerate(POOL_WINDOWS):
        lo = g * POOL_GROUP_DIM
        while back < win:
            run = run + st_ref[POOL_STATE - back]
            back += 1
        sums.append(run[:, lo:lo + POOL_GROUP_DIM])
        inv_cnt.append(1.0 / win)
    y = _pool_groups(sums, h, inv_cnt, pw_ref, ps_ref)
    x1 = x + gate1_ref[...] * y
    o_ref[...] = _ffn_tail(x1, n2_ref[...], shift2_ref[...], scale2_ref[...], gate2_ref[...], wg_ref, wu_ref, wd_ref)
    st_out_ref[0:POOL_STATE - 1] = st_ref[1:POOL_STATE]
    st_out_ref[POOL_STATE - 1] = h


def _pool_sample_call(x, state_t, n1, pool_w, pool_scale, n2, mods, ffn):
    bd = x.shape[0]
    g = len(POOL_WINDOWS)
    mod = lambda c: pl.BlockSpec((bd, D_MODEL), lambda i: (0, c), pipeline_mode=pl.Buffered(1))
    return pl.pallas_call(
        _pool_sample_body,
        grid=(1,),
        in_specs=[_resident((bd, D_MODEL)), _resident((POOL_STATE, bd, D_MODEL)), _resident((1, D_MODEL)),
                  mod(0), mod(1), mod(2), _resident((g, POOL_GROUP_DIM, POOL_GROUP_DIM)), _resident((1, D_MODEL)),
                  _resident((1, D_MODEL)), mod(3), mod(4), mod(5)] + _ffn_specs(ffn[0]),
        out_specs=(pl.BlockSpec((bd, D_MODEL), lambda i: (0, 0)),
                   pl.BlockSpec((POOL_STATE, bd, D_MODEL), lambda i: (0, 0, 0))),
        out_shape=(jax.ShapeDtypeStruct((bd, D_MODEL), F32), jax.ShapeDtypeStruct((POOL_STATE, bd, D_MODEL), F32)),
        compiler_params=_params("arbitrary"),
        name="pool_sample",
    )(x, state_t, n1, mods, mods, mods, pool_w, pool_scale, n2, mods, mods, mods, *ffn[1:])


def _gla_gate(hb, wa1_ref, wa2_ref, ba):
    a1 = jnp.dot(hb, wa1_ref[...], preferred_element_type=F32).astype(BF16)
    return _log_sigmoid(jnp.dot(a1, wa2_ref[...], preferred_element_type=F32) + ba) / GLA_TAU


def _gla_proj_body(x_ref, n1_ref, shift_ref, scale_ref, wq_ref, wk_ref, wv_ref, wa1_ref, wa2_ref, ba_ref, wr_ref,
                   q_ref, k_ref, v_ref, g_ref, r_ref):
    hb = _prenorm(x_ref[...], n1_ref[...], shift_ref[...], scale_ref[...]).astype(BF16)
    q_ref[...] = jnp.dot(hb, wq_ref[...], preferred_element_type=F32) * (GLA_DK ** -0.5)
    k_ref[...] = jnp.dot(hb, wk_ref[...], preferred_element_type=F32)
    v_ref[...] = jnp.dot(hb, wv_ref[...], preferred_element_type=F32).astype(v_ref.dtype)
    g_ref[...] = _gla_gate(hb, wa1_ref, wa2_ref, ba_ref[...])
    r_ref[...] = _silu(jnp.dot(hb, wr_ref[...], preferred_element_type=F32))


def _gla_proj_call(x, n1, rs, w):
    kw = GLA_HEADS * GLA_DK
    rank = w["wa1"].shape[1]
    return pl.pallas_call(
        _gla_proj_body,
        grid=rs.grid,
        in_specs=[rs.rows(D_MODEL), _resident((1, D_MODEL)), rs.mod(0), rs.mod(1),
                  _resident((D_MODEL, kw)), _resident((D_MODEL, kw)), _resident((D_MODEL, D_MODEL)),
                  _resident((D_MODEL, rank)), _resident((rank, kw)), _resident((1, kw)), _resident((D_MODEL, D_MODEL))],
        out_specs=(rs.rows(kw), rs.rows(kw), rs.rows(D_MODEL), rs.rows(kw), rs.rows(D_MODEL)),
        out_shape=(rs.shape(kw), rs.shape(kw), rs.shape(D_MODEL, BF16), rs.shape(kw), rs.shape(D_MODEL)),
        compiler_params=_params("arbitrary", "arbitrary"),
        name="gla_proj",
    )(x, n1, rs.mods, rs.mods, w["wq"], w["wk"], w["wv"], w["wa1"], w["wa2"], w["ba"], w["wr"])


def _gla_out_norm(o, gn, r):
    return _rms(o, gn) * r


def _gla_scan_body(q_ref, k_ref, v_ref, g_ref, r_ref, gn_ref, y_ref, s_out_ref, st_ref):
    batch, rows = q_ref.shape[0], q_ref.shape[1]
    c = GLA_CHUNK

    @pl.when(pl.program_id(0) == 0)
    def _():
        st_ref[...] = jnp.zeros_like(st_ref)

    ri = lax.broadcasted_iota(jnp.int32, (c, c), 0)
    ci = lax.broadcasted_iota(jnp.int32, (c, c), 1)
    causal = ci <= ri
    tril = causal.astype(BF16)
    gn = gn_ref[...]

    def chunk(n, carry):
        r0 = pl.multiple_of(n * c, c)
        for b in range(batch):
            q, k, g = (ref[b, pl.ds(r0, c), :] for ref in (q_ref, k_ref, g_ref))
            v, rr = v_ref[b, pl.ds(r0, c), :], r_ref[b, pl.ds(r0, c), :]
            bcum = _dot3_left(tril, g)
            b_last = bcum[c - 1:c, :]
            e = jnp.exp(bcum)
            qe = (q * e).astype(BF16)
            ke = (k * jnp.exp(-bcum)).astype(BF16)
            kd = (k * jnp.exp(b_last - bcum)).astype(BF16)
            decay = jnp.exp(b_last)
            for h in range(GLA_HEADS):
                ks = slice(h * GLA_DK, (h + 1) * GLA_DK)
                vs = slice(h * GLA_DV, (h + 1) * GLA_DV)
                st = st_ref[b, h]
                a = lax.dot_general(qe[:, ks], ke[:, ks], NT_DIMS, preferred_element_type=F32)
                a = jnp.where(causal, a, 0.0).astype(BF16)
                o = (jnp.dot(a, v[:, vs], preferred_element_type=F32)
                     + lax.dot_general(qe[:, ks], st.astype(BF16), NT_DIMS, preferred_element_type=F32))
                st_ref[b, h] = st * decay[:, ks] + lax.dot_general(v[:, vs], kd[:, ks], TN_DIMS,
                                                                  preferred_element_type=F32)
                y_ref[b, pl.ds(r0, c), vs] = _gla_out_norm(o, gn, rr[:, vs]).astype(BF16)
        return carry

    lax.fori_loop(0, rows // c, chunk, 0)

    @pl.when(pl.program_id(0) == pl.num_programs(0) - 1)
    def _():
        for b in range(batch):
            for h in range(GLA_HEADS):
                s_out_ref[b, h] = st_ref[b, h].T


def _gla_scan_call(q, k, v, g, r, gn, rows):
    batch, length, kw = q.shape
    blk = lambda cols: pl.BlockSpec((batch, rows, cols), lambda i: (0, i, 0))
    s_shape = (batch, GLA_HEADS, GLA_DK, GLA_DV)
    return pl.pallas_call(
        _gla_scan_body,
        grid=(length // rows,),
        in_specs=[blk(kw), blk(kw), blk(D_MODEL), blk(kw), blk(D_MODEL), _resident((1, GLA_DV))],
        out_specs=(blk(D_MODEL), pl.BlockSpec(s_shape, lambda i: (0, 0, 0, 0))),
        out_shape=(jax.ShapeDtypeStruct((batch, length, D_MODEL), BF16), jax.ShapeDtypeStruct(s_shape, F32)),
        scratch_shapes=[pltpu.VMEM((batch, GLA_HEADS, GLA_DV, GLA_DK), F32)],
        compiler_params=_params("arbitrary"),
        name="gla_scan",
    )(q, k, v, g, r, gn)


def _gla_sample_proj_body(x_ref, n1_ref, shift_ref, scale_ref, wq_ref, wk_ref, wv_ref, wa1_ref, wa2_ref, ba_ref,
                          wr_ref, q_ref, k_ref, v_ref, g_ref, r_ref):
    hb = _prenorm(x_ref[...], n1_ref[...], shift_ref[...], scale_ref[...]).astype(BF16)
    q_ref[...] = jnp.dot(hb, wq_ref[...], preferred_element_type=F32).T * (GLA_DK ** -0.5)
    k_ref[...] = jnp.dot(hb, wk_ref[...], preferred_element_type=F32).T
    v_ref[...] = jnp.dot(hb, wv_ref[...], preferred_element_type=F32)
    g_ref[...] = _gla_gate(hb, wa1_ref, wa2_ref, ba_ref[...]).T
    r_ref[...] = _silu(jnp.dot(hb, wr_ref[...], preferred_element_type=F32))


def _gla_sample_proj_call(x, n1, mods, w):
    bd = x.shape[0]
    kw = GLA_HEADS * GLA_DK
    rank = w["wa1"].shape[1]
    mod = lambda c: pl.BlockSpec((bd, D_MODEL), lambda i: (0, c), pipeline_mode=pl.Buffered(1))
    full = lambda *s: pl.BlockSpec(s, lambda i: (0,) * len(s))
    return pl.pallas_call(
        _gla_sample_proj_body,
        grid=(1,),
        in_specs=[_resident((bd, D_MODEL)), _resident((1, D_MODEL)), mod(0), mod(1),
                  _resident((D_MODEL, kw)), _resident((D_MODEL, kw)), _resident((D_MODEL, D_MODEL)),
                  _resident((D_MODEL, rank)), _resident((rank, kw)), _resident((1, kw)), _resident((D_MODEL, D_MODEL))],
        out_specs=(full(kw, bd), full(kw, bd), full(bd, D_MODEL), full(kw, bd), full(bd, D_MODEL)),
        out_shape=(jax.ShapeDtypeStruct((kw, bd), F32), jax.ShapeDtypeStruct((kw, bd), F32),
                   jax.ShapeDtypeStruct((bd, D_MODEL), F32), jax.ShapeDtypeStruct((kw, bd), F32),
                   jax.ShapeDtypeStruct((bd, D_MODEL), F32)),
        compiler_params=_params("arbitrary"),
        name="gla_sample_proj",
    )(x, n1, mods, mods, w["wq"], w["wk"], w["wv"], w["wa1"], w["wa2"], w["ba"], w["wr"])


def _gla_decode_body(q_ref, k_ref, g_ref, v_ref, r_ref, gn_ref, s_ref, y_ref, s_out_ref):
    b = pl.program_id(0)
    n = q_ref.shape[1]
    sel = (lax.broadcasted_iota(jnp.int32, (n, n), 0) == b).astype(BF16)
    wide = lambda col: jnp.concatenate([col, col], axis=-1).reshape(GLA_HEADS, GLA_DK, GLA_DV)
    qcol = wide(_lane_column(q_ref[...], sel))
    kcol = wide(_lane_column(k_ref[...], sel))
    decay = wide(jnp.exp(_lane_column(g_ref[...], sel)))
    gn = gn_ref[...]
    for h in range(GLA_HEADS):
        vs = slice(h * GLA_DV, (h + 1) * GLA_DV)
        s_new = s_ref[h] * decay[h] + kcol[h] * v_ref[:, vs]
        s_out_ref[h] = s_new
        o = jnp.sum(qcol[h] * s_new, axis=0, keepdims=True)
        y_ref[:, vs] = _gla_out_norm(o, gn, r_ref[:, vs])


def _gla_decode_call(q_t, k_t, g_t, v, r, gn, state):
    kw, bd = q_t.shape
    assert bd == V7X_LANES, "the sample batch is laid out on the 128 lanes"
    full = lambda *s: pl.BlockSpec(s, lambda i: (0,) * len(s))
    row = pl.BlockSpec((None, 1, D_MODEL), lambda i: (i, 0, 0))
    st = pl.BlockSpec((None, GLA_HEADS, GLA_DK, GLA_DV), lambda i: (i, 0, 0, 0))
    return pl.pallas_call(
        _gla_decode_body,
        grid=(bd,),
        in_specs=[full(kw, bd), full(kw, bd), full(kw, bd), row, row, full(1, GLA_DV), st],
        out_specs=(row, st),
        out_shape=(jax.ShapeDtypeStruct((bd, 1, D_MODEL), F32), jax.ShapeDtypeStruct(state.shape, F32)),
        compiler_params=_params("arbitrary"),
        name="gla_decode",
    )(q_t, k_t, g_t, v, r, gn, state)


def kernel(x_prompt, x_sample, cache_k, cache_v, cache_logf, state_pool, state_gla, page_table, c_prompt, c_sample, norm1, norm2, ada_w, ada_b, fox_wq, fox_wk, fox_wv, fox_wf, fox_bf, fox_qn, fox_kn, fox_wo, pool_w, pool_scale, gla_wq, gla_wk, gla_wv, gla_wa1, gla_wa2, gla_ba, gla_wr, gla_gn, gla_wo, ffn_wg, ffn_wu, ffn_wd):
    batch, length, d = x_prompt.shape
    bd = x_sample.shape[0]
    depth = norm1.shape[0]
    assert d == D_MODEL and x_sample.shape[1] == 1

    c_all = jnp.concatenate([c_prompt, c_sample], axis=0)
    c_all = jnp.pad(c_all, ((0, -c_all.shape[0] % 8), (0, 0)))
    mods = _ada_call(c_all, ada_w, ada_b)

    ck_t = jnp.transpose(cache_k, (0, 1, 3, 4, 2))
    cv_t = jnp.transpose(cache_v, (0, 1, 3, 4, 2))
    clf_t = jnp.transpose(cache_logf, (0, 1, 3, 2))

    xp = x_prompt
    xs = x_sample.reshape(1, bd, d)
    ffn_all = (ffn_wg.astype(BF16), ffn_wu.astype(BF16), ffn_wd.astype(BF16))
    n_fox = (depth + N_MIXERS - 1) // N_MIXERS
    stack_shape = (n_fox, batch, FOX_HEADS, FOX_HEAD_DIM, length)
    kv_stacks = (jnp.zeros(stack_shape, F32), jnp.ones(stack_shape, F32))
    outs = {k: [] for k in ("lf_p", "k_s", "v_s", "lf_s", "pool_p", "pool_s", "gla_p", "gla_s")}
    for i in range(depth):
        kind, j = i % N_MIXERS, i // N_MIXERS
        rp = _Rows(mods[i, :batch], batch, length, ROW_BLOCK)
        rsm = _Rows(mods[i, batch:batch + bd], 1, bd, bd, per_row=True)
        n1, n2 = norm1[i][None], norm2[i][None]
        ffn = (i,) + ffn_all
        if kind == 0:
            w = dict(wq_t=fox_wq[j].T.astype(BF16), wk_t=fox_wk[j].T.astype(BF16), wv_t=fox_wv[j].T.astype(BF16),
                     wf_t=fox_wf[j].T.astype(BF16), bf=fox_bf[j][:, None], qn=fox_qn[j][:, None], kn=fox_kn[j][:, None])
            wo = fox_wo[j].astype(BF16)
            qs_t, ks_t, vs_t, lfs_t = _fox_sample_proj_call(xs, n1, rsm, w)
            ys = _fox_decode_call(page_table, ck_t, cv_t, clf_t, j, qs_t[0], ks_t[0], vs_t[0], lfs_t[0])
            qa, kr, *kv_stacks, lf_t = _fox_prompt_proj_call(xp, n1, rp, w, j, kv_stacks)
            yp = _fox_attn_call(qa, kr, kv_stacks[1], j)
            outs["lf_p"].append(jnp.transpose(lf_t, (0, 2, 1)))
            outs["k_s"].append(jnp.transpose(ks_t, (3, 0, 1, 2)))
            outs["v_s"].append(jnp.transpose(vs_t, (3, 0, 1, 2)))
            outs["lf_s"].append(jnp.transpose(lfs_t, (2, 0, 1)))
            xp = _out_ffn_call(xp, yp, wo, n2, rp, ffn)
            xs = _out_ffn_call(xs, ys[None], wo, n2, rsm, ffn)
        elif kind == 1:
            pw, ps = pool_w[j].astype(BF16), pool_scale[j][None]
            xp, tail = _pool_ffn_call(xp, n1, pw, ps, n2, _Rows(mods[i, :batch], batch, length, POOL_ROW_BLOCK), ffn)
            st_t = jnp.transpose(state_pool[j], (1, 0, 2))
            xs2, st_new = _pool_sample_call(xs[0], st_t, n1, pw, ps, n2, rsm.mods, ffn)
            xs = xs2[None]
            outs["pool_p"].append(tail[:, POOL_HALO - POOL_STATE:])
            outs["pool_s"].append(jnp.transpose(st_new, (1, 0, 2)))
        else:
            w = dict(wq=gla_wq[j].astype(BF16), wk=gla_wk[j].astype(BF16), wv=gla_wv[j].astype(BF16),
                     wa1=gla_wa1[j].astype(BF16), wa2=gla_wa2[j].astype(BF16), ba=gla_ba[j][None],
                     wr=gla_wr[j].astype(BF16))
            wo, gn = gla_wo[j].astype(BF16), gla_gn[j][None]
            q, k, v, g, r = _gla_proj_call(xp, n1, rp, w)
            yp, s_fin = _gla_scan_call(q, k, v, g, r, gn, 4 * GLA_CHUNK)
            q_t, k_t, vs_, g_t, r_s = _gla_sample_proj_call(xs[0], n1, rsm.mods, w)
            ys, s_new = _gla_decode_call(q_t, k_t, g_t, vs_[:, None], r_s[:, None], gn, state_gla[j])
            outs["gla_p"].append(s_fin)
            outs["gla_s"].append(s_new)
            xp = _out_ffn_call(xp, yp, wo, n2, rp, ffn)
            xs = _out_ffn_call(xs, ys.reshape(1, bd, d).astype(BF16), wo, n2, rsm, ffn)

    stack = lambda name: jnp.stack(outs[name])
    return (xp, xs.reshape(bd, 1, d),
            jnp.transpose(kv_stacks[0], (0, 1, 4, 2, 3)), jnp.transpose(kv_stacks[1], (0, 1, 4, 2, 3)), stack("lf_p"),
            stack("k_s"), stack("v_s"), stack("lf_s"),
            stack("pool_p"), stack("pool_s"), stack("gla_p"), stack("gla_s"))
```
